```python
import math
import jax
import jax.numpy as jnp
from jax import lax
import numpy as np

D_MODEL = 4096
BATCH = 32
SEQ = 256
DEPTH = 2
DEC_BATCH = 2
DEC_SEQ = 1024
PAST_LEN = 256

GRID_W = 64
EPS = 1e-6
ML_HEADS = 8
ML_DK = 128
ML_DV = 128
ML_CHUNK = 64
ML_WIDTH = ML_HEADS * ML_DV
DA_HEADS = 8
DA_DH = 64
DA_VD = 2 * DA_DH
DA_QK = DA_HEADS * 2 * DA_DH
DA_WIDTH = DA_HEADS * DA_VD
ROPE_BASE = 10000.0
Q_BLOCK = 128
HY_WIDTH = 1024
HY_ORDER = 2
HY_CONV = 3
HY_BANDS = 8
HY_EMB = 1 + 2 * HY_BANDS
HY_FH = 64
HY_TARGET = 1e-2
HY_DECAY_PCT_NEAR = 0.3
HY_DECAY_PCT_FAR = 1.5
HY_MIN_DECAY = math.log(HY_TARGET) / HY_DECAY_PCT_FAR
HY_MAX_DECAY = math.log(HY_TARGET) / HY_DECAY_PCT_NEAR
N_BRANCH = 3
N_EXPERTS = 32
TOP_K = 4
D_EXPERT = D_MODEL // 4
SWIGLU_ALPHA = 1.702
SWIGLU_LIMIT = 7.0
IN_SIZES = (ML_HEADS * ML_DK, ML_HEADS * ML_DK, ML_WIDTH, ML_WIDTH, 2 * ML_HEADS, 2 * ML_HEADS,
            DA_QK, DA_QK, DA_WIDTH, 3 * HY_WIDTH, N_BRANCH * D_MODEL)
N_IN = sum(IN_SIZES)

kernel_name = 'hybrid_mlstm_diffattn_hyena_moe_diffusion_step'


def rmsnorm(x, g):
    xf = x.astype(jnp.float32)
    y = xf * lax.rsqrt(jnp.mean(xf * xf, axis=-1, keepdims=True) + EPS)
    return (y * g.astype(jnp.float32)).astype(x.dtype)


def rope_axis(x, pos):
    half = x.shape[-1] // 2
    inv = ROPE_BASE ** (-jnp.arange(half, dtype=jnp.float32) / half)
    ang = pos.astype(jnp.float32)[:, None] * inv
    cos = jnp.cos(ang)[:, None, None, :]
    sin = jnp.sin(ang)[:, None, None, :]
    x1, x2 = x[..., :half], x[..., half:]
    return jnp.concatenate([x1 * cos - x2 * sin, x1 * sin + x2 * cos], axis=-1)


def axial_rope(x, row, col):
    xf = x.astype(jnp.float32)
    r = DA_DH // 2
    out = jnp.concatenate([rope_axis(xf[..., :r], row), rope_axis(xf[..., r:], col)], axis=-1)
    return out.astype(x.dtype)


def diff_attention(q, k, v, lam):
    B, Sq, H = q.shape[:3]
    nb = Sq // Q_BLOCK
    qb = jnp.moveaxis(q.reshape((B, nb, Q_BLOCK) + q.shape[2:]), 1, 0)
    scale = DA_DH ** -0.5

    def one_block(qblk):
        s = jnp.einsum('bqhcd,bkhcd->bhcqk', qblk, k, preferred_element_type=jnp.float32) * scale
        p = jax.nn.softmax(s, axis=-1)
        a = p[:, :, 0] - lam * p[:, :, 1]
        return jnp.einsum('bhqk,bkhv->bqhv', a.astype(v.dtype), v)

    o = lax.map(one_block, qb)
    return jnp.moveaxis(o, 0, 1).reshape(B, Sq, H, v.shape[-1])


def mlstm_chunkwise(q, k, v, i_pre, log_f, C0, n0, m0):
    B, L, H, _ = q.shape
    nc = L // ML_CHUNK

    def to_chunks(a):
        a = jnp.moveaxis(a, 1, 2)
        a = a.reshape(a.shape[:2] + (nc, ML_CHUNK) + a.shape[3:])
        return jnp.moveaxis(a, 2, 0)

    lower = jnp.tril(jnp.ones((ML_CHUNK, ML_CHUNK), dtype=bool))

    def step(carry, xs):
        C, n, m = carry
        qc, kc, vc, ic, fc = xs
        b = jnp.cumsum(fc, axis=-1)
        log_d = jnp.where(lower, b[..., :, None] - b[..., None, :] + ic[..., None, :], -jnp.inf)
        m_inter = b + m[..., None]
        m_s = jnp.maximum(m_inter, jnp.max(log_d, axis=-1))
        w_intra = jnp.einsum('bhsk,bhjk->bhsj', qc, kc) * jnp.exp(log_d - m_s[..., None])
        w_inter = jnp.exp(m_inter - m_s)
        num = jnp.einsum('bhsj,bhjv->bhsv', w_intra, vc) + w_inter[..., None] * jnp.einsum('bhsk,bhkv->bhsv', qc, C)
        den = jnp.sum(w_intra, axis=-1) + w_inter * jnp.einsum('bhsk,bhk->bhs', qc, n)
        h = num / jnp.maximum(jnp.abs(den), jnp.exp(-m_s))[..., None]
        b_end = b[..., -1]
        log_w = b_end[..., None] - b + ic
        m_new = jnp.maximum(b_end + m, jnp.max(log_w, axis=-1))
        w = jnp.exp(log_w - m_new[..., None])
        carry_scale = jnp.exp(b_end + m - m_new)
        C_new = carry_scale[..., None, None] * C + jnp.einsum('bhjk,bhjv->bhkv', kc * w[..., None], vc)
        n_new = carry_scale[..., None] * n + jnp.einsum('bhj,bhjk->bhk', w, kc)
        return (C_new, n_new, m_new), h

    xs = (to_chunks(q), to_chunks(k), to_chunks(v), to_chunks(i_pre), to_chunks(log_f))
    (C_T, n_T, m_T), h = lax.scan(step, (C0, n0, m0), xs)
    h = jnp.moveaxis(h, 0, 2).reshape(B, H, L, -1)
    return jnp.swapaxes(h, 1, 2), C_T, n_T, m_T


def mlstm_bidir(q, k, v, i_pre, log_f, C0, n0, m0):
    h_f, C_f, n_f, m_f = mlstm_chunkwise(q, k, v, i_pre[:, :, 0], log_f[:, :, 0], C0[:, 0], n0[:, 0], m0[:, 0])
    rev = lambda a: jnp.flip(a, axis=1)
    h_b, C_b, n_b, m_b = mlstm_chunkwise(rev(q), rev(k), rev(v), rev(i_pre[:, :, 1]), rev(log_f[:, :, 1]),
                                         C0[:, 1], n0[:, 1], m0[:, 1])
    h = h_f + rev(h_b)
    return h, jnp.stack([C_f, C_b], axis=1), jnp.stack([n_f, n_b], axis=1), jnp.stack([m_f, m_b], axis=1)


def hyena_filters(L, w1, b1, freq, w2, b2, w3):
    f32 = jnp.float32
    pos = jnp.arange(L, dtype=f32)
    t = pos / (L - 1)
    ang = (2.0 * math.pi * pos / L)[:, None] * jnp.linspace(1e-4, HY_BANDS - 1, HY_BANDS, dtype=f32)
    feats = jnp.concatenate([t[:, None], jnp.cos(ang), -jnp.sin(ang)], axis=-1)
    hdn = jnp.sin(freq[0] * (feats @ w1 + b1))
    hdn = jnp.sin(freq[1] * (hdn @ w2 + b2))
    filt = (hdn @ w3).reshape(L, HY_ORDER, HY_WIDTH)
    offset = jnp.abs(pos - L // 2) / (L // 2)
    deltas = jnp.abs(jnp.linspace(HY_MIN_DECAY, HY_MAX_DECAY, HY_WIDTH, dtype=f32))
    window = jnp.exp(-offset[:, None] * deltas)
    return filt * window[:, None, :]


def fft_conv_centred(z, h):
    L = z.shape[1]
    n = 2 * L
    Z = jnp.fft.rfft(z, n=n, axis=1)
    Hf = jnp.fft.rfft(h, n=n, axis=0)
    y = jnp.fft.irfft(Z * Hf[None], n=n, axis=1)
    return lax.slice_in_dim(y, L // 2, L // 2 + L, axis=1)


def hyena_branch(u, conv_w, conv_b, w1, b1, freq, w2, b2, w3, d_skip):
    L = u.shape[1]
    uc = lax.conv_general_dilated(u, conv_w[:, None, :], window_strides=(1,),
                                  padding=[(HY_CONV // 2, HY_CONV // 2)],
                                  dimension_numbers=('NWC', 'WIO', 'NWC'),
                                  feature_group_count=u.shape[-1]) + conv_b
    v, x1, x2 = jnp.split(uc.astype(jnp.float32), 3, axis=-1)
    f32 = jnp.float32
    filt = hyena_filters(L, w1.astype(f32), b1.astype(f32), freq.astype(f32), w2.astype(f32),
                         b2.astype(f32), w3.astype(f32))
    d = d_skip.astype(f32)
    z = v
    for o, gate in enumerate((x1, x2)):
        z = gate * (fft_conv_centred(z, filt[:, o]) + z * d[o])
    return z.astype(u.dtype)


def moe_ffn(h, router_w, router_b, w_gate, b_gate, w_up, b_up, w_down, b_down):
    f32 = jnp.float32
    logits = jnp.matmul(h, router_w, preferred_element_type=f32) + router_b.astype(f32)
    top_v, top_i = lax.top_k(logits, TOP_K)
    weights = jax.nn.softmax(top_v, axis=-1)
    gates = jnp.einsum('nk,nke->ne', weights, jax.nn.one_hot(top_i, N_EXPERTS, dtype=f32)).astype(h.dtype)
    y = jnp.zeros_like(h)
    for e in range(N_EXPERTS):
        a_glu = jnp.minimum(h @ w_gate[e] + b_gate[e], SWIGLU_LIMIT)
        a_lin = jnp.clip(h @ w_up[e] + b_up[e], -SWIGLU_LIMIT, SWIGLU_LIMIT)
        act = a_glu * jax.nn.sigmoid(SWIGLU_ALPHA * a_glu) * (a_lin + 1.0)
        y = y + gates[:, e:e + 1] * (act @ w_down[e] + b_down[e])
    return y


def trunk_layer(x, cond, lam_init, p, ctx, pos):
    f32 = jnp.float32
    B, L, _ = x.shape
    mod = jax.nn.silu(cond) @ p['ada_w'] + p['ada_b']
    mod = mod.reshape(mod.shape[0], 1, 6, D_MODEL)
    sh1, sc1, g1, sh2, sc2, g2 = (mod[:, :, j] for j in range(6))
    h = rmsnorm(x, p['norm1_g']) * (1 + sc1) + sh1
    u = h @ p['w_in']
    points = np.cumsum(IN_SIZES)[:-1].tolist()
    mq, mk, mv, mo, mi, mf, dq, dk, dv, hu, gl = jnp.split(u, points, axis=-1)

    q = mq.reshape(B, L, ML_HEADS, ML_DK).astype(f32)
    k = mk.reshape(B, L, ML_HEADS, ML_DK).astype(f32) * (ML_DK ** -0.5)
    v = mv.reshape(B, L, ML_HEADS, ML_DV).astype(f32)
    i_pre = mi.reshape(B, L, 2, ML_HEADS).astype(f32) + p['ml_b_i'].astype(f32)
    log_f = jax.nn.log_sigmoid(mf.reshape(B, L, 2, ML_HEADS).astype(f32) + p['ml_b_f'].astype(f32))
    if ctx is None:
        C0 = jnp.zeros((B, 2, ML_HEADS, ML_DK, ML_DV), f32)
        n0 = jnp.zeros((B, 2, ML_HEADS, ML_DK), f32)
        m0 = jnp.zeros((B, 2, ML_HEADS), f32)
    else:
        C0, n0, m0 = ctx[2].astype(f32), ctx[3].astype(f32), ctx[4].astype(f32)
    h_ml, C_T, n_T, m_T = mlstm_bidir(q, k, v, i_pre, log_f, C0, n0, m0)
    h_ml = jax.nn.sigmoid(mo.reshape(B, L, ML_HEADS, ML_DV).astype(f32)) * h_ml
    h_ml = rmsnorm(h_ml, p['ml_norm_g'].reshape(ML_HEADS, ML_DV)).astype(x.dtype).reshape(B, L, ML_WIDTH)

    qd = dq.reshape(B, L, DA_HEADS, 2, DA_DH)
    kd = dk.reshape(B, L, DA_HEADS, 2, DA_DH)
    vd = dv.reshape(B, L, DA_HEADS, DA_VD)
    if ctx is None:
        keys, vals = kd, vd
    else:
        row, col = pos
        qd = axial_rope(qd, row, col)
        keys = jnp.concatenate([ctx[0].astype(x.dtype), axial_rope(kd, row, col)], axis=1)
        vals = jnp.concatenate([ctx[1].astype(x.dtype), vd], axis=1)
    lp = p['da_lam'].astype(f32)
    lam = jnp.exp(jnp.sum(lp[0] * lp[1])) - jnp.exp(jnp.sum(lp[2] * lp[3])) + lam_init
    o = diff_attention(qd, keys, vals, lam)
    o = (rmsnorm(o, p['da_subln_g']) * (1.0 - lam_init)).reshape(B, L, DA_WIDTH)

    h_hy = hyena_branch(hu, p['hy_conv_w'], p['hy_conv_b'], p['hy_f_w1'], p['hy_f_b1'], p['hy_f_freq'],
                        p['hy_f_w2'], p['hy_f_b2'], p['hy_f_w3'], p['hy_d'])

    gates = jax.nn.sigmoid(gl.reshape(B, L, N_BRANCH, D_MODEL) + p['b_merge'])
    merged = (gates[:, :, 0] * (h_ml @ p['w_br_ml']) + gates[:, :, 1] * (o @ p['w_br_da'])
              + gates[:, :, 2] * (h_hy @ p['w_br_hy']))
    x = x + g1 * (merged @ p['w_out'])

    h2 = rmsnorm(x, p['norm2_g']) * (1 + sc2) + sh2
    y = moe_ffn(h2.reshape(B * L, D_MODEL), p['router_w'], p['router_b'], p['moe_w_gate'], p['moe_b_gate'],
                p['moe_w_up'], p['moe_b_up'], p['moe_w_down'], p['moe_b_down'])
    x = x + g2 * y.reshape(B, L, D_MODEL)
    return x, (kd, vd, C_T.astype(x.dtype), n_T.astype(x.dtype), m_T.astype(x.dtype))


def setup_inputs(seed: int = 0) -> dict:
    key = jax.random.key(seed)
    keys = list(jax.random.split(key, 48))
    f32 = jnp.float32
    D = D_MODEL

    def nrm(shape, scale=1.0):
        return jax.random.normal(keys.pop(), shape, f32) * scale

    def gain(shape):
        return 1.0 + nrm(shape, 0.02)

    inp = {}
    inp['x_prompt'] = nrm((BATCH, SEQ, D))
    inp['x_sample'] = nrm((DEC_BATCH, DEC_SEQ, D))
    inp['cache_attn_k'] = nrm((DEC_BATCH, DEPTH, PAST_LEN, DA_HEADS, 2, DA_DH))
    inp['cache_attn_v'] = nrm((DEC_BATCH, DEPTH, PAST_LEN, DA_HEADS, DA_VD))
    inp['state_mlstm_C'] = nrm((DEC_BATCH, DEPTH, 2, ML_HEADS, ML_DK, ML_DV), 0.1)
    inp['state_mlstm_n'] = nrm((DEC_BATCH, DEPTH, 2, ML_HEADS, ML_DK), 0.1)
    inp['state_mlstm_m'] = nrm((DEC_BATCH, DEPTH, 2, ML_HEADS))
    inp['c'] = nrm((DEC_BATCH, D))
    inp['c_ctx'] = nrm((D,))
    inp['ada_w'] = nrm((DEPTH, D, 6 * D), 0.5 * D ** -0.5)
    inp['ada_b'] = nrm((DEPTH, 6 * D), 0.01)
    inp['norm1_g'] = gain((DEPTH, D))
    inp['norm2_g'] = gain((DEPTH, D))
    inp['w_in'] = nrm((DEPTH, D, N_IN), D ** -0.5)
    inp['ml_b_i'] = nrm((DEPTH, 2, ML_HEADS), 0.1)
    inp['ml_b_f'] = jax.random.uniform(keys.pop(), (DEPTH, 2, ML_HEADS), f32, 3.0, 6.0)
    inp['ml_norm_g'] = gain((DEPTH, ML_WIDTH))
    inp['da_lam'] = nrm((DEPTH, 4, DA_DH), 0.1)
    inp['da_subln_g'] = gain((DEPTH, DA_VD))
    inp['hy_conv_w'] = nrm((DEPTH, HY_CONV, 3 * HY_WIDTH), HY_CONV ** -0.5)
    inp['hy_conv_b'] = nrm((DEPTH, 3 * HY_WIDTH), 0.01)
    inp['hy_f_w1'] = nrm((DEPTH, HY_EMB, HY_FH), HY_EMB ** -0.5)
    inp['hy_f_b1'] = nrm((DEPTH, HY_FH), 0.1)
    inp['hy_f_freq'] = gain((DEPTH, 2, HY_FH))
    inp['hy_f_w2'] = nrm((DEPTH, HY_FH, HY_FH), HY_FH ** -0.5)
    inp['hy_f_b2'] = nrm((DEPTH, HY_FH), 0.1)
    inp['hy_f_w3'] = nrm((DEPTH, HY_FH, HY_ORDER * HY_WIDTH), 0.02 * HY_FH ** -0.5)
    inp['hy_d'] = nrm((DEPTH, HY_ORDER, HY_WIDTH))
    inp['w_br_ml'] = nrm((DEPTH, ML_WIDTH, D), ML_WIDTH ** -0.5)
    inp['w_br_da'] = nrm((DEPTH, DA_WIDTH, D), DA_WIDTH ** -0.5)
    inp['w_br_hy'] = nrm((DEPTH, HY_WIDTH, D), HY_WIDTH ** -0.5)
    inp['b_merge'] = nrm((DEPTH, N_BRANCH, D), 0.01)
    inp['w_out'] = nrm((DEPTH, D, D), D ** -0.5)
    inp['router_w'] = nrm((DEPTH, D, N_EXPERTS), D ** -0.5)
    inp['router_b'] = nrm((DEPTH, N_EXPERTS), 0.01)
    inp['moe_w_gate'] = nrm((DEPTH, N_EXPERTS, D, D_EXPERT), D ** -0.5)
    inp['moe_b_gate'] = nrm((DEPTH, N_EXPERTS, D_EXPERT), 0.01)
    inp['moe_w_up'] = nrm((DEPTH, N_EXPERTS, D, D_EXPERT), D ** -0.5)
    inp['moe_b_up'] = nrm((DEPTH, N_EXPERTS, D_EXPERT), 0.01)
    inp['moe_w_down'] = nrm((DEPTH, N_EXPERTS, D_EXPERT, D), D_EXPERT ** -0.5)
    inp['moe_b_down'] = nrm((DEPTH, N_EXPERTS, D), 0.01)
    inp['final_g'] = gain((D,))
    return inp


def reference(x_prompt, x_sample, cache_attn_k, cache_attn_v, state_mlstm_C, state_mlstm_n, state_mlstm_m,
              c, c_ctx, ada_w, ada_b, norm1_g, norm2_g, w_in, ml_b_i, ml_b_f, ml_norm_g, da_lam, da_subln_g,
              hy_conv_w, hy_conv_b, hy_f_w1, hy_f_b1, hy_f_freq, hy_f_w2, hy_f_b2, hy_f_w3, hy_d,
              w_br_ml, w_br_da, w_br_hy, b_merge, w_out, router_w, router_b,
              moe_w_gate, moe_b_gate, moe_w_up, moe_b_up, moe_w_down, moe_b_down, final_g):
    def layer_params(l):
        return {'ada_w': ada_w[l], 'ada_b': ada_b[l], 'norm1_g': norm1_g[l], 'norm2_g': norm2_g[l],
                'w_in': w_in[l], 'ml_b_i': ml_b_i[l], 'ml_b_f': ml_b_f[l], 'ml_norm_g': ml_norm_g[l],
                'da_lam': da_lam[l], 'da_subln_g': da_subln_g[l], 'hy_conv_w': hy_conv_w[l],
                'hy_conv_b': hy_conv_b[l], 'hy_f_w1': hy_f_w1[l], 'hy_f_b1': hy_f_b1[l],
                'hy_f_freq': hy_f_freq[l], 'hy_f_w2': hy_f_w2[l], 'hy_f_b2': hy_f_b2[l],
                'hy_f_w3': hy_f_w3[l], 'hy_d': hy_d[l], 'w_br_ml': w_br_ml[l], 'w_br_da': w_br_da[l],
                'w_br_hy': w_br_hy[l], 'b_merge': b_merge[l], 'w_out': w_out[l], 'router_w': router_w[l],
                'router_b': router_b[l], 'moe_w_gate': moe_w_gate[l], 'moe_b_gate': moe_b_gate[l],
                'moe_w_up': moe_w_up[l], 'moe_b_up': moe_b_up[l], 'moe_w_down': moe_w_down[l],
                'moe_b_down': moe_b_down[l]}

    xp = x_prompt
    ks, vs, Cs, ns, ms = [], [], [], [], []
    for l in range(DEPTH):
        lam_init = 0.8 - 0.6 * math.exp(-0.3 * l)
        xp, (k_l, v_l, C_l, n_l, m_l) = trunk_layer(xp, c_ctx[None], lam_init, layer_params(l), None, None)
        ks.append(k_l)
        vs.append(v_l)
        Cs.append(C_l)
        ns.append(n_l)
        ms.append(m_l)
    y_prompt = rmsnorm(xp, final_g)
    new_attn_k = jnp.stack(ks, axis=1)
    new_attn_v = jnp.stack(vs, axis=1)
    new_mlstm_C = jnp.stack(Cs, axis=1)
    new_mlstm_n = jnp.stack(ns, axis=1)
    new_mlstm_m = jnp.stack(ms, axis=1)

    n_lat = x_sample.shape[1]
    rows = n_lat // GRID_W
    row = jnp.repeat(jnp.arange(rows), GRID_W)
    col = jnp.arange(n_lat) % GRID_W
    xs = x_sample
    for l in range(DEPTH):
        lam_init = 0.8 - 0.6 * math.exp(-0.3 * l)
        ctx = (cache_attn_k[:, l], cache_attn_v[:, l], state_mlstm_C[:, l], state_mlstm_n[:, l],
               state_mlstm_m[:, l])
        xs, _ = trunk_layer(xs, c, lam_init, layer_params(l), ctx, (row, col))
    y_sample = rmsnorm(xs, final_g)
    return (y_prompt, y_sample, new_attn_k, new_attn_v, new_mlstm_C, new_mlstm_n, new_mlstm_m)
```

```python
import functools
import math

import jax
import jax.numpy as jnp
from jax import lax
from jax.experimental import pallas as pl
from jax.experimental.pallas import tpu as pltpu

f32 = jnp.float32
bf16 = jnp.bfloat16

D_MODEL = 4096
BATCH = 32
SEQ = 256
DEPTH = 2
DEC_BATCH = 2
DEC_SEQ = 1024
PAST_LEN = 256
GRID_W = 64
EPS = 1e-6
ML_HEADS = 8
ML_DK = 128
ML_DV = 128
DA_HEADS = 8
DA_DH = 64
DA_VD = 128
ROPE_BASE = 10000.0
HY_WIDTH = 1024
HY_BANDS = 8
HY_EMB = 1 + 2 * HY_BANDS
HY_FH = 64
HY_TARGET = 1e-2
HY_MIN_DECAY = math.log(HY_TARGET) / 1.5
HY_MAX_DECAY = math.log(HY_TARGET) / 0.3
N_BRANCH = 3
N_EXPERTS = 32
TOP_K = 4
D_EXPERT = D_MODEL // 4
SWIGLU_ALPHA = 1.702
SWIGLU_LIMIT = 7.0

T_CTX = BATCH * SEQ
T_LAT = DEC_BATCH * DEC_SEQ
T_ALL = T_CTX + T_LAT
N_COND = 1 + DEC_BATCH
COND_PAD = 8

OFF_MQ, OFF_MK, OFF_MV, OFF_MO = 0, 1024, 2048, 3072
OFF_DQ, OFF_DK, OFF_DV = 4096, 5120, 6144
OFF_HU = 7168
OFF_GL = 10240
OFF_GATE = 22528
N_IN_PAD = 22784

LANE = 128
VMEM_LIMIT = 56 * 1024 * 1024

ML_CHUNK = 256
MOE_TILE = 512
MOE_NT = (T_ALL * TOP_K) // MOE_TILE + N_EXPERTS
MOE_ROWS = MOE_NT * MOE_TILE


def _params(n_axes):
    return pltpu.CompilerParams(dimension_semantics=("arbitrary",) * n_axes,
                                vmem_limit_bytes=VMEM_LIMIT)


def _cond_of_rows(row0):
    return jnp.where(row0 < T_CTX, 0, 1 + (row0 - T_CTX) // DEC_SEQ)


def _ada_kernel(c_ref, w_ref, b_ref, o_ref):
    c = c_ref[...]
    s = c * jax.nn.sigmoid(c)
    o_ref[...] = jnp.dot(s.astype(bf16), w_ref[...].astype(bf16),
                         preferred_element_type=f32) + b_ref[...]


def ada_modulation(cond, ada_w, ada_b):
    tn = 1024
    n = 6 * D_MODEL
    return pl.pallas_call(
        _ada_kernel,
        out_shape=jax.ShapeDtypeStruct((DEPTH, COND_PAD, n), f32),
        grid=(DEPTH, n // tn),
        in_specs=[pl.BlockSpec((COND_PAD, D_MODEL), lambda l, j: (0, 0)),
                  pl.BlockSpec((None, D_MODEL, tn), lambda l, j: (l, 0, j)),
                  pl.BlockSpec((None, 1, tn), lambda l, j: (l, 0, j))],
        out_specs=pl.BlockSpec((None, COND_PAD, tn), lambda l, j: (l, 0, j)),
        compiler_params=_params(2), name="ada_mod",
    )(cond, ada_w, ada_b.reshape(DEPTH, 1, n))


def _norm_mod_kernel(x_ref, g_ref, sc_ref, sh_ref, o_ref):
    x = x_ref[...]
    y = x * lax.rsqrt(jnp.mean(x * x, axis=-1, keepdims=True) + EPS) * g_ref[...]
    o_ref[...] = (y * (1.0 + sc_ref[...]) + sh_ref[...]).astype(o_ref.dtype)


def norm_mod(x, g, mod, j_scale, j_shift):
    tm = 512
    return pl.pallas_call(
        _norm_mod_kernel,
        out_shape=jax.ShapeDtypeStruct((T_ALL, D_MODEL), bf16),
        grid=(T_ALL // tm,),
        in_specs=[pl.BlockSpec((tm, D_MODEL), lambda i: (i, 0)),
                  pl.BlockSpec((1, D_MODEL), lambda i: (0, 0)),
                  pl.BlockSpec((None, 1, D_MODEL), lambda i: (_cond_of_rows(i * tm) * 6 + j_scale, 0, 0)),
                  pl.BlockSpec((None, 1, D_MODEL), lambda i: (_cond_of_rows(i * tm) * 6 + j_shift, 0, 0))],
        out_specs=pl.BlockSpec((tm, D_MODEL), lambda i: (i, 0)),
        compiler_params=_params(1), name="norm_mod",
    )(x, g.reshape(1, D_MODEL), mod, mod)


def _final_norm_kernel(x_ref, g_ref, o_ref):
    x = x_ref[...]
    o_ref[...] = x * lax.rsqrt(jnp.mean(x * x, axis=-1, keepdims=True) + EPS) * g_ref[...]


def final_norm(x, g):
    tm = 512
    return pl.pallas_call(
        _final_norm_kernel,
        out_shape=jax.ShapeDtypeStruct((T_ALL, D_MODEL), f32),
        grid=(T_ALL // tm,),
        in_specs=[pl.BlockSpec((tm, D_MODEL), lambda i: (i, 0)),
                  pl.BlockSpec((1, D_MODEL), lambda i: (0, 0))],
        out_specs=pl.BlockSpec((tm, D_MODEL), lambda i: (i, 0)),
        compiler_params=_params(1), name="final_norm",
    )(x, g.reshape(1, D_MODEL))


def _mm_kernel(a_ref, w_ref, o_ref):
    o_ref[...] = jnp.dot(a_ref[...], w_ref[...], preferred_element_type=f32)


def in_projection(h, w_p, layer):
    tm, tn = 2048, 256
    return pl.pallas_call(
        _mm_kernel,
        out_shape=jax.ShapeDtypeStruct((T_ALL, N_IN_PAD), f32),
        grid=(T_ALL // tm, N_IN_PAD // tn),
        in_specs=[pl.BlockSpec((tm, D_MODEL), lambda i, j: (i, 0)),
                  pl.BlockSpec((None, D_MODEL, tn), lambda i, j: (layer, 0, j))],
        out_specs=pl.BlockSpec((tm, tn), lambda i, j: (i, j)),
        compiler_params=_params(2), name="in_proj",
    )(h, w_p)


def _log_sigmoid(x):
    return jnp.minimum(x, 0.0) - jnp.log(1.0 + jnp.exp(-jnp.abs(x)))


def _mlstm_chunk(q, k, v, i_col, f_col, i_row, f_row, C, n, m, mask_self, mask_other):
    b_col = jnp.sum(jnp.where(mask_self, f_row, 0.0), axis=1, keepdims=True)
    b_row = jnp.sum(jnp.where(mask_other, f_col, 0.0), axis=0, keepdims=True)
    total = jnp.sum(f_col, axis=0, keepdims=True)
    log_d = jnp.where(mask_self, b_col - b_row + i_row, -jnp.inf)
    m_inter = b_col + m
    m_s = jnp.maximum(m_inter, jnp.max(log_d, axis=1, keepdims=True))
    qb = q.astype(bf16)
    kb = k.astype(bf16)
    vb = v.astype(bf16)
    s = lax.dot_general(qb, kb, (((1,), (1,)), ((), ())), preferred_element_type=f32)
    w_intra = s * jnp.exp(log_d - m_s)
    w_inter = jnp.exp(m_inter - m_s)
    num = (jnp.dot(w_intra.astype(bf16), vb, preferred_element_type=f32)
           + w_inter * jnp.dot(qb, C.astype(bf16), preferred_element_type=f32))
    den = jnp.sum(w_intra, axis=1, keepdims=True) + w_inter * jnp.sum(q * n, axis=1, keepdims=True)
    h = num / jnp.maximum(jnp.abs(den), jnp.exp(-m_s))
    log_w = total - b_col + i_col
    m_new = jnp.maximum(total + m, jnp.max(log_w, axis=0, keepdims=True))
    w = jnp.exp(log_w - m_new)
    carry = jnp.exp(total + m - m_new)
    kw = k * w
    C_new = carry * C + lax.dot_general(kw.astype(bf16), vb, (((0,), (0,)), ((), ())),
                                        preferred_element_type=f32)
    n_new = carry * n + jnp.sum(kw, axis=0, keepdims=True)
    return h, C_new, n_new, m_new


def _mlstm_kernel(*refs, nchunks, zero_init):
    if zero_init:
        (q_ref, k_ref, v_ref, o_ref, gc_ref, gr_ref, bc_ref, br_ref, ng_ref,
         h_out, C_out, n_out, m_out, hacc) = refs
    else:
        (q_ref, k_ref, v_ref, o_ref, gc_ref, gr_ref, bc_ref, br_ref, ng_ref, C0_ref, n0_ref, m0_ref,
         h_out, C_out, n_out, m_out, hacc) = refs
    CH = ML_CHUNK
    gc = gc_ref[...] + bc_ref[...]
    gr = gr_ref[...] + br_ref[...]
    row = lax.broadcasted_iota(jnp.int32, (CH, CH), 0)
    col = lax.broadcasted_iota(jnp.int32, (CH, CH), 1)
    lower = col <= row
    upper = col >= row
    scale = ML_DK ** -0.5
    for d in range(2):
        if zero_init:
            C = jnp.zeros((ML_DK, ML_DV), f32)
            n = jnp.zeros((1, ML_DK), f32)
            m = jnp.zeros((1, 1), f32)
        else:
            C = C0_ref[d]
            n = n0_ref[d]
            m = m0_ref[d][:, 0:1]
        mask_self, mask_other = (lower, upper) if d == 0 else (upper, lower)
        order = range(nchunks) if d == 0 else range(nchunks - 1, -1, -1)
        for c in order:
            sl = slice(c * CH, (c + 1) * CH)
            q = q_ref[sl, :]
            k = k_ref[sl, :] * scale
            v = v_ref[sl, :]
            i_col = gc[sl, d:d + 1]
            f_col = _log_sigmoid(gc[sl, 2 + d:3 + d])
            i_row = gr[d:d + 1, sl]
            f_row = _log_sigmoid(gr[2 + d:3 + d, sl])
            h, C, n, m = _mlstm_chunk(q, k, v, i_col, f_col, i_row, f_row, C, n, m, mask_self, mask_other)
            if d == 0:
                hacc[sl, :] = h
            else:
                hacc[sl, :] += h
        C_out[d] = C
        n_out[d:d + 1, :] = n
        m_out[d:d + 1, :] = jnp.broadcast_to(m, (1, LANE))
    hg = jax.nn.sigmoid(o_ref[...]) * hacc[...]
    y = hg * lax.rsqrt(jnp.mean(hg * hg, axis=-1, keepdims=True) + EPS) * ng_ref[...]
    h_out[...] = y.astype(h_out.dtype)


def mlstm(u, gcol, grow, bcol, brow, norm_g, *, nseq, L, row_blk0, init=None):
    H = ML_HEADS
    nchunks = L // ML_CHUNK
    cb = lambda off: off // LANE
    in_specs = [
        pl.BlockSpec((L, LANE), lambda b, h: (row_blk0 + b, cb(OFF_MQ) + h)),
        pl.BlockSpec((L, LANE), lambda b, h: (row_blk0 + b, cb(OFF_MK) + h)),
        pl.BlockSpec((L, LANE), lambda b, h: (row_blk0 + b, cb(OFF_MV) + h)),
        pl.BlockSpec((L, LANE), lambda b, h: (row_blk0 + b, cb(OFF_MO) + h)),
        pl.BlockSpec((None, L, 4), lambda b, h: (h, row_blk0 + b, 0)),
        pl.BlockSpec((None, 4, L), lambda b, h: (h, 0, row_blk0 + b)),
        pl.BlockSpec((None, 1, 4), lambda b, h: (h, 0, 0)),
        pl.BlockSpec((None, 4, 1), lambda b, h: (h, 0, 0)),
        pl.BlockSpec((None, 1, LANE), lambda b, h: (h, 0, 0)),
    ]
    args = [u, u, u, u, gcol, grow, bcol, brow, norm_g]
    if init is not None:
        C0, n0, m0 = init
        in_specs += [
            pl.BlockSpec((None, 2, None, ML_DK, ML_DV), lambda b, h: (b, 0, h, 0, 0)),
            pl.BlockSpec((None, 2, None, 1, ML_DK), lambda b, h: (b, 0, h, 0, 0)),
            pl.BlockSpec((None, 2, None, 1, LANE), lambda b, h: (b, 0, h, 0, 0)),
        ]
        args += [C0, n0, m0]
    out_shape = (jax.ShapeDtypeStruct((nseq * L, ML_HEADS * ML_DV), bf16),
                 jax.ShapeDtypeStruct((nseq, H, 2, ML_DK, ML_DV), f32),
                 jax.ShapeDtypeStruct((nseq, H, 2, ML_DK), f32),
                 jax.ShapeDtypeStruct((nseq, H, 2, LANE), f32))
    out_specs = (pl.BlockSpec((L, LANE), lambda b, h: (b, h)),
                 pl.BlockSpec((None, None, 2, ML_DK, ML_DV), lambda b, h: (b, h, 0, 0, 0)),
                 pl.BlockSpec((None, None, 2, ML_DK), lambda b, h: (b, h, 0, 0)),
                 pl.BlockSpec((None, None, 2, LANE), lambda b, h: (b, h, 0, 0)))
    return pl.pallas_call(
        functools.partial(_mlstm_kernel, nchunks=nchunks, zero_init=init is None),
        out_shape=out_shape, grid=(nseq, H), in_specs=in_specs, out_specs=out_specs,
        scratch_shapes=[pltpu.VMEM((L, ML_DV), f32)],
        compiler_params=_params(2), name="mlstm_ctx" if init is None else "mlstm_lat",
    )(*args)


def _rope(x, cos, sin_signed):
    lane = lax.broadcasted_iota(jnp.int32, x.shape, 1)
    partner = jnp.where((lane & 16) == 0, pltpu.roll(x, LANE - 16, axis=1), pltpu.roll(x, 16, axis=1))
    return x * cos + partner * sin_signed


def _attn_kernel(*refs, n_cache, lam_init):
    if n_cache:
        (q_ref, k_ref, v_ref, lam_ref, g_ref, ck_ref, cv_ref, cq_ref, sq_ref, ckk_ref, skk_ref,
         o_ref, k_all, v_all) = refs
    else:
        q_ref, k_ref, v_ref, lam_ref, g_ref, o_ref, k_all, v_all = refs

    @pl.when(pl.program_id(2) == 0)
    def _():
        if n_cache:
            k_all[0:n_cache, :] = ck_ref[...].astype(bf16)
            v_all[0:n_cache, :] = cv_ref[...].astype(bf16)
            k_all[n_cache:, :] = _rope(k_ref[...], ckk_ref[...], skk_ref[...]).astype(bf16)
        else:
            k_all[...] = k_ref[...].astype(bf16)
        v_all[n_cache:, :] = v_ref[...].astype(bf16)

    q = q_ref[...]
    if n_cache:
        q = _rope(q, cq_ref[...], sq_ref[...])
    q = q * (DA_DH ** -0.5)
    lane = lax.broadcasted_iota(jnp.int32, q.shape, 1)
    lp = lam_ref[...]
    lam = (jnp.exp(jnp.sum(lp[0:1] * lp[1:2], axis=1, keepdims=True))
           - jnp.exp(jnp.sum(lp[2:3] * lp[3:4], axis=1, keepdims=True)) + lam_init)
    kb = k_all[...]
    probs = []
    for c in range(2):
        qc = jnp.where((lane >= c * DA_DH) & (lane < (c + 1) * DA_DH), q, 0.0).astype(bf16)
        s = lax.dot_general(qc, kb, (((1,), (1,)), ((), ())), preferred_element_type=f32)
        p = jnp.exp(s - jnp.max(s, axis=1, keepdims=True))
        probs.append(p / jnp.sum(p, axis=1, keepdims=True))
    a = probs[0] - lam * probs[1]
    o = jnp.dot(a.astype(bf16), v_all[...], preferred_element_type=f32)
    y = o * lax.rsqrt(jnp.mean(o * o, axis=-1, keepdims=True) + EPS) * g_ref[...]
    o_ref[...] = (y * (1.0 - lam_init)).astype(o_ref.dtype)


def diff_attention(u, da_lam, subln_g, lam_init, *, layer, nseq, L, row_blk0, tq, cache=None, rope=None):
    nq = L // tq
    cb = lambda off: off // LANE
    n_cache = 0 if cache is None else PAST_LEN
    in_specs = [
        pl.BlockSpec((tq, LANE), lambda b, h, i: ((row_blk0 + b) * nq + i, cb(OFF_DQ) + h)),
        pl.BlockSpec((L, LANE), lambda b, h, i: (row_blk0 + b, cb(OFF_DK) + h)),
        pl.BlockSpec((L, LANE), lambda b, h, i: (row_blk0 + b, cb(OFF_DV) + h)),
        pl.BlockSpec((None, 4, DA_DH), lambda b, h, i: (layer, 0, 0)),
        pl.BlockSpec((None, 1, DA_VD), lambda b, h, i: (layer, 0, 0)),
    ]
    args = [u, u, u, da_lam, subln_g.reshape(DEPTH, 1, DA_VD)]
    if cache is not None:
        ck, cv = cache
        cos, sin = rope
        in_specs += [
            pl.BlockSpec((None, None, PAST_LEN, LANE), lambda b, h, i: (b, layer, 0, h)),
            pl.BlockSpec((None, None, PAST_LEN, LANE), lambda b, h, i: (b, layer, 0, h)),
            pl.BlockSpec((tq, LANE), lambda b, h, i: (i, 0)),
            pl.BlockSpec((tq, LANE), lambda b, h, i: (i, 0)),
            pl.BlockSpec((L, LANE), lambda b, h, i: (0, 0)),
            pl.BlockSpec((L, LANE), lambda b, h, i: (0, 0)),
        ]
        args += [ck, cv, cos, sin, cos, sin]
    return pl.pallas_call(
        functools.partial(_attn_kernel, n_cache=n_cache, lam_init=lam_init),
        out_shape=jax.ShapeDtypeStruct((nseq * L, DA_HEADS * DA_VD), bf16),
        grid=(nseq, DA_HEADS, nq), in_specs=in_specs,
        out_specs=pl.BlockSpec((tq, LANE), lambda b, h, i: (b * nq + i, h)),
        scratch_shapes=[pltpu.VMEM((n_cache + L, LANE), bf16), pltpu.VMEM((n_cache + L, LANE), bf16)],
        compiler_params=_params(3), name="attn_ctx" if cache is None else "attn_lat",
    )(*args)


def rope_tables(L):
    pos = jnp.arange(L)
    row = (pos // GRID_W).astype(f32)
    colp = (pos % GRID_W).astype(f32)
    lane = jnp.arange(LANE)
    d = lane % DA_DH
    half = DA_DH // 4
    inv = ROPE_BASE ** (-(d % half).astype(f32) / half)
    p = jnp.where((d < DA_DH // 2)[None, :], row[:, None], colp[:, None])
    ang = p * inv[None, :]
    sign = jnp.where((d % (2 * half)) < half, -1.0, 1.0)
    return jnp.cos(ang), jnp.sin(ang) * sign[None, :]


def dft_matrices(L):
    n = 2 * L
    nf = L + 1
    nfp = L + 16
    k = jnp.arange(nfp)
    valid = (k < nf)
    s = jnp.arange(L)
    ang_f = (2.0 * math.pi / n) * ((k[:, None] * s[None, :]) % n).astype(f32)
    fr = jnp.where(valid[:, None], jnp.cos(ang_f), 0.0)
    fi = jnp.where(valid[:, None], -jnp.sin(ang_f), 0.0)
    fwd = jnp.concatenate([fr, fi], axis=0)
    t = jnp.arange(L) + L // 2
    ang_i = (2.0 * math.pi / n) * ((t[:, None] * k[None, :]) % n).astype(f32)
    wk = jnp.where((k == 0) | (k == L), 1.0, 2.0) / n
    wk = jnp.where(valid, wk, 0.0)
    inv = jnp.concatenate([jnp.cos(ang_i) * wk[None, :], -jnp.sin(ang_i) * wk[None, :]], axis=1)
    return fwd, inv


def _hy_filter_kernel(feat_ref, w1_ref, b1_ref, fq_ref, w2_ref, b2_ref, w3_ref, off_ref, dl_ref, fwd_ref, o_ref):
    hp = lax.Precision.HIGHEST
    fq = fq_ref[...]
    hdn = jnp.sin(fq[0:1] * (jnp.dot(feat_ref[...], w1_ref[...], precision=hp, preferred_element_type=f32)
                             + b1_ref[...]))
    hdn = jnp.sin(fq[1:2] * (jnp.dot(hdn, w2_ref[...], precision=hp, preferred_element_type=f32) + b2_ref[...]))
    filt = jnp.dot(hdn, w3_ref[...], precision=hp, preferred_element_type=f32)
    filt = filt * jnp.exp(-off_ref[...] * dl_ref[...])
    o_ref[...] = jnp.dot(fwd_ref[...], filt, precision=hp, preferred_element_type=f32)


def hyena_filter_spectrum(L, fwd, w1, b1, fq, w2, b2, w3):
    nfp2 = fwd.shape[0]
    pos = jnp.arange(L, dtype=f32)
    t = pos / (L - 1)
    ang = (2.0 * math.pi * pos / L)[:, None] * jnp.linspace(1e-4, HY_BANDS - 1, HY_BANDS, dtype=f32)
    feats = jnp.concatenate([t[:, None], jnp.cos(ang), -jnp.sin(ang)], axis=-1)
    feats = jnp.pad(feats, ((0, 0), (0, LANE - HY_EMB)))
    offset = (jnp.abs(pos - L // 2) / (L // 2))[:, None]
    deltas = jnp.abs(jnp.linspace(HY_MIN_DECAY, HY_MAX_DECAY, HY_WIDTH, dtype=f32))
    deltas = jnp.concatenate([deltas, deltas])[None, :]
    cbw = 256
    nw = 2 * HY_WIDTH
    full = lambda shape: pl.BlockSpec(shape, lambda j: (0,) * len(shape))
    return pl.pallas_call(
        _hy_filter_kernel,
        out_shape=jax.ShapeDtypeStruct((nfp2, nw), f32),
        grid=(nw // cbw,),
        in_specs=[full((L, LANE)), full((LANE, LANE)), full((1, LANE)), full((2, LANE)), full((LANE, LANE)),
                  full((1, LANE)), pl.BlockSpec((LANE, cbw), lambda j: (0, j)), full((L, 1)),
                  pl.BlockSpec((1, cbw), lambda j: (0, j)), full((nfp2, L))],
        out_specs=pl.BlockSpec((nfp2, cbw), lambda j: (0, j)),
        compiler_params=_params(1), name="hyena_filter",
    )(feats, w1, b1, fq, w2, b2, w3, offset, deltas, fwd)


def _hyena_kernel(v_ref, x1_ref, x2_ref, cw_ref, cbias_ref, hf_ref, d_ref, fwd_ref, inv_ref, o_ref, *, L):
    nfp = fwd_ref.shape[0] // 2
    cb = v_ref.shape[1]
    rows = lax.broadcasted_iota(jnp.int32, (L, cb), 0)

    def conv3(x_ref, part):
        x = x_ref[...]
        w = cw_ref[part]
        prev = jnp.where(rows == 0, 0.0, pltpu.roll(x, 1, axis=0))
        nxt = jnp.where(rows == L - 1, 0.0, pltpu.roll(x, L - 1, axis=0))
        return w[0:1] * prev + w[1:2] * x + w[2:3] * nxt + cbias_ref[part]

    z = conv3(v_ref, 0)
    gates = (conv3(x1_ref, 1), conv3(x2_ref, 2))
    fwd = fwd_ref[...]
    inv = inv_ref[...]
    dsk = d_ref[...]
    for o in range(2):
        zf = jnp.dot(fwd, z.astype(bf16), preferred_element_type=f32)
        zr, zi = zf[:nfp], zf[nfp:]
        hr, hi = hf_ref[o, :nfp, :], hf_ref[o, nfp:, :]
        yf = jnp.concatenate([zr * hr - zi * hi, zr * hi + zi * hr], axis=0).astype(bf16)
        y = jnp.dot(inv, yf, preferred_element_type=f32)
        z = gates[o] * (y + z * dsk[o:o + 1])
    o_ref[...] = z.astype(o_ref.dtype)


def hyena(u, conv_w, conv_b, hf, d_skip, fwd, inv, *, nseq, L, row_blk0):
    cbw = 256
    ncb = HY_WIDTH // cbw
    nfp2 = fwd.shape[0]
    ub = lambda part: (lambda b, j: (row_blk0 + b, (OFF_HU + part * HY_WIDTH) // cbw + j))
    return pl.pallas_call(
        functools.partial(_hyena_kernel, L=L),
        out_shape=jax.ShapeDtypeStruct((nseq * L, HY_WIDTH), bf16),
        grid=(nseq, ncb),
        in_specs=[pl.BlockSpec((L, cbw), ub(0)), pl.BlockSpec((L, cbw), ub(1)), pl.BlockSpec((L, cbw), ub(2)),
                  pl.BlockSpec((3, 3, cbw), lambda b, j: (0, 0, j)),
                  pl.BlockSpec((3, 1, cbw), lambda b, j: (0, 0, j)),
                  pl.BlockSpec((2, nfp2, cbw), lambda b, j: (0, 0, j)),
                  pl.BlockSpec((2, cbw), lambda b, j: (0, j)),
                  pl.BlockSpec((nfp2, L), lambda b, j: (0, 0)),
                  pl.BlockSpec((L, nfp2), lambda b, j: (0, 0))],
        out_specs=pl.BlockSpec((L, cbw), lambda b, j: (b, j)),
        compiler_params=_params(2), name="hyena_ctx" if L == SEQ else "hyena_lat",
    )(u, u, u, conv_w, conv_b, hf, d_skip, fwd.astype(bf16), inv.astype(bf16))


def _merge_kernel(a0, a1, a2, w0, w1, w2, g0, g1, g2, bm_ref, o_ref):
    acc = None
    for j, (a, w, g) in enumerate(((a0, w0, g0), (a1, w1, g1), (a2, w2, g2))):
        gate = jax.nn.sigmoid(g[...] + bm_ref[j])
        term = gate * jnp.dot(a[...], w[...].astype(bf16), preferred_element_type=f32)
        acc = term if acc is None else acc + term
    o_ref[...] = acc.astype(o_ref.dtype)


def merge_branches(h_ml, h_da, h_hy, w_ml, w_da, w_hy, u, b_merge, layer):
    tm, tn = 512, 512
    a_spec = pl.BlockSpec((tm, 1024), lambda i, j: (i, 0))
    w_spec = pl.BlockSpec((None, 1024, tn), lambda i, j: (layer, 0, j))
    g_spec = lambda br: pl.BlockSpec((tm, tn), lambda i, j: (i, (OFF_GL + br * D_MODEL) // tn + j))
    return pl.pallas_call(
        _merge_kernel,
        out_shape=jax.ShapeDtypeStruct((T_ALL, D_MODEL), bf16),
        grid=(T_ALL // tm, D_MODEL // tn),
        in_specs=[a_spec, a_spec, a_spec, w_spec, w_spec, w_spec, g_spec(0), g_spec(1), g_spec(2),
                  pl.BlockSpec((None, N_BRANCH, 1, tn), lambda i, j: (layer, 0, 0, j))],
        out_specs=pl.BlockSpec((tm, tn), lambda i, j: (i, j)),
        compiler_params=_params(2), name="merge",
    )(h_ml, h_da, h_hy, w_ml, w_da, w_hy, u, u, u, b_merge.reshape(DEPTH, N_BRANCH, 1, D_MODEL))


def _out_proj_kernel(a_ref, w_ref, x_ref, g_ref, o_ref):
    o_ref[...] = x_ref[...] + g_ref[...] * jnp.dot(a_ref[...], w_ref[...].astype(bf16),
                                                   preferred_element_type=f32)


def out_projection(merged, w_out, x, mod, layer):
    tm, tn = 1024, 256
    return pl.pallas_call(
        _out_proj_kernel,
        out_shape=jax.ShapeDtypeStruct((T_ALL, D_MODEL), f32),
        grid=(T_ALL // tm, D_MODEL // tn),
        in_specs=[pl.BlockSpec((tm, D_MODEL), lambda i, j: (i, 0)),
                  pl.BlockSpec((None, D_MODEL, tn), lambda i, j: (layer, 0, j)),
                  pl.BlockSpec((tm, tn), lambda i, j: (i, j)),
                  pl.BlockSpec((None, 1, tn), lambda i, j: (_cond_of_rows(i * tm) * 6 + 2, 0, j))],
        out_specs=pl.BlockSpec((tm, tn), lambda i, j: (i, j)),
        compiler_params=_params(2), name="out_proj",
    )(merged, w_out, x, mod)


def _norm_router_kernel(x_ref, g_ref, sc_ref, sh_ref, rw_ref, rb_ref, h_ref, ti_ref, tw_ref):
    x = x_ref[...]
    y = x * lax.rsqrt(jnp.mean(x * x, axis=-1, keepdims=True) + EPS) * g_ref[...]
    h = y * (1.0 + sc_ref[...]) + sh_ref[...]
    h_ref[...] = h.astype(h_ref.dtype)
    logits = jnp.dot(h, rw_ref[...], precision=lax.Precision.HIGHEST, preferred_element_type=f32) + rb_ref[...]
    lane = lax.broadcasted_iota(jnp.int32, logits.shape, 1)
    logits = jnp.where(lane < N_EXPERTS, logits, -jnp.inf)
    idx_out = jnp.zeros(logits.shape, jnp.int32)
    val_out = jnp.zeros(logits.shape, f32)
    top = None
    for k in range(TOP_K):
        mx = jnp.max(logits, axis=1, keepdims=True)
        idx = jnp.min(jnp.where(logits == mx, lane, LANE), axis=1, keepdims=True)
        if top is None:
            top = mx
        idx_out = jnp.where(lane == k, idx, idx_out)
        val_out = jnp.where(lane == k, jnp.exp(mx - top), val_out)
        logits = jnp.where(lane == idx, -jnp.inf, logits)
    ti_ref[...] = idx_out
    tw_ref[...] = val_out / jnp.sum(val_out, axis=1, keepdims=True)


def norm_router(x, g, mod, router_w, router_b):
    tm = 256
    rw = jnp.pad(router_w, ((0, 0), (0, LANE - N_EXPERTS)))
    rb = jnp.pad(router_b, (0, LANE - N_EXPERTS)).reshape(1, LANE)
    row = pl.BlockSpec((tm, LANE), lambda i: (i, 0))
    return pl.pallas_call(
        _norm_router_kernel,
        out_shape=(jax.ShapeDtypeStruct((T_ALL, D_MODEL), bf16),
                   jax.ShapeDtypeStruct((T_ALL, LANE), jnp.int32),
                   jax.ShapeDtypeStruct((T_ALL, LANE), f32)),
        grid=(T_ALL // tm,),
        in_specs=[pl.BlockSpec((tm, D_MODEL), lambda i: (i, 0)),
                  pl.BlockSpec((1, D_MODEL), lambda i: (0, 0)),
                  pl.BlockSpec((None, 1, D_MODEL), lambda i: (_cond_of_rows(i * tm) * 6 + 4, 0, 0)),
                  pl.BlockSpec((None, 1, D_MODEL), lambda i: (_cond_of_rows(i * tm) * 6 + 3, 0, 0)),
                  pl.BlockSpec((D_MODEL, LANE), lambda i: (0, 0)),
                  pl.BlockSpec((1, LANE), lambda i: (0, 0))],
        out_specs=(pl.BlockSpec((tm, D_MODEL), lambda i: (i, 0)), row, row),
        compiler_params=_params(1), name="norm_router",
    )(x, g.reshape(1, D_MODEL), mod, mod, rw, rb)


def route(top_i):
    eid = top_i.reshape(-1)
    onehot = (eid[:, None] == jnp.arange(N_EXPERTS)[None, :]).astype(jnp.int32)
    rank = jnp.take_along_axis(jnp.cumsum(onehot, axis=0) - onehot, eid[:, None], axis=1)[:, 0]
    counts = jnp.sum(onehot, axis=0)
    padded = ((counts + MOE_TILE - 1) // MOE_TILE) * MOE_TILE
    ends = jnp.cumsum(padded)
    dest = (ends - padded)[eid] + rank
    tok_of_slot = jnp.zeros((MOE_ROWS,), jnp.int32).at[dest].set(jnp.arange(eid.shape[0], dtype=jnp.int32) // TOP_K)
    tile_expert = jnp.minimum(jnp.searchsorted(ends, jnp.arange(MOE_NT) * MOE_TILE, side="right"),
                              N_EXPERTS - 1).astype(jnp.int32)
    n_used = (ends[-1] // MOE_TILE).astype(jnp.int32).reshape(1)
    return dest, tok_of_slot, tile_expert, n_used


def _expert_up_kernel(te_ref, nu_ref, x_ref, wg_ref, bg_ref, wu_ref, bu_ref, o_ref):
    @pl.when(pl.program_id(1) < nu_ref[0])
    def _():
        x = x_ref[...]
        a_glu = jnp.dot(x, wg_ref[...].astype(bf16), preferred_element_type=f32) + bg_ref[...]
        a_lin = jnp.dot(x, wu_ref[...].astype(bf16), preferred_element_type=f32) + bu_ref[...]
        a_glu = jnp.minimum(a_glu, SWIGLU_LIMIT)
        a_lin = jnp.clip(a_lin, -SWIGLU_LIMIT, SWIGLU_LIMIT)
        o_ref[...] = (a_glu * jax.nn.sigmoid(SWIGLU_ALPHA * a_glu) * (a_lin + 1.0)).astype(o_ref.dtype)


def expert_up(xs, w_gate, b_gate, w_up, b_up, tile_expert, n_used, layer):
    tn = 256
    tile = lambda j, r, te, nu: jnp.minimum(r, nu[0] - 1)
    w_spec = pl.BlockSpec((None, None, D_MODEL, tn), lambda j, r, te, nu: (layer, te[tile(j, r, te, nu)], 0, j))
    b_spec = pl.BlockSpec((None, None, 1, tn), lambda j, r, te, nu: (layer, te[tile(j, r, te, nu)], 0, j))
    return pl.pallas_call(
        _expert_up_kernel,
        out_shape=jax.ShapeDtypeStruct((MOE_ROWS, D_EXPERT), bf16),
        grid_spec=pltpu.PrefetchScalarGridSpec(
            num_scalar_prefetch=2, grid=(D_EXPERT // tn, MOE_NT),
            in_specs=[pl.BlockSpec((MOE_TILE, D_MODEL), lambda j, r, te, nu: (tile(j, r, te, nu), 0)),
                      w_spec, b_spec, w_spec, b_spec],
            out_specs=pl.BlockSpec((MOE_TILE, tn), lambda j, r, te, nu: (tile(j, r, te, nu), j))),
        compiler_params=_params(2), name="expert_up",
    )(tile_expert, n_used, xs, w_gate, b_gate.reshape(DEPTH, N_EXPERTS, 1, D_EXPERT),
      w_up, b_up.reshape(DEPTH, N_EXPERTS, 1, D_EXPERT))


def _expert_down_kernel(te_ref, nu_ref, a_ref, w_ref, b_ref, gs_ref, o_ref):
    @pl.when(pl.program_id(1) < nu_ref[0])
    def _():
        y = jnp.dot(a_ref[...], w_ref[...].astype(bf16), preferred_element_type=f32) + b_ref[...]
        o_ref[...] = gs_ref[...] * y


def expert_down(act, w_down, b_down, gate_sorted, tile_expert, n_used, layer):
    tn = 1024
    tile = lambda j, r, te, nu: jnp.minimum(r, nu[0] - 1)
    return pl.pallas_call(
        _expert_down_kernel,
        out_shape=jax.ShapeDtypeStruct((MOE_ROWS, D_MODEL), f32),
        grid_spec=pltpu.PrefetchScalarGridSpec(
            num_scalar_prefetch=2, grid=(D_MODEL // tn, MOE_NT),
            in_specs=[pl.BlockSpec((MOE_TILE, D_EXPERT), lambda j, r, te, nu: (tile(j, r, te, nu), 0)),
                      pl.BlockSpec((None, None, D_EXPERT, tn),
                                   lambda j, r, te, nu: (layer, te[tile(j, r, te, nu)], 0, j)),
                      pl.BlockSpec((None, None, 1, tn), lambda j, r, te, nu: (layer, te[tile(j, r, te, nu)], 0, j)),
                      pl.BlockSpec((MOE_TILE, 1), lambda j, r, te, nu: (tile(j, r, te, nu), 0))],
            out_specs=pl.BlockSpec((MOE_TILE, tn), lambda j, r, te, nu: (tile(j, r, te, nu), j))),
        compiler_params=_params(2), name="expert_down",
    )(tile_expert, n_used, act, w_down, b_down.reshape(DEPTH, N_EXPERTS, 1, D_MODEL), gate_sorted)


def _combine_kernel(x_ref, y_ref, g_ref, o_ref):
    y = y_ref[:, 0:D_MODEL]
    for k in range(1, TOP_K):
        y = y + y_ref[:, k * D_MODEL:(k + 1) * D_MODEL]
    o_ref[...] = x_ref[...] + g_ref[...] * y


def moe_combine(x, y4, mod):
    tm = 128
    return pl.pallas_call(
        _combine_kernel,
        out_shape=jax.ShapeDtypeStruct((T_ALL, D_MODEL), f32),
        grid=(T_ALL // tm,),
        in_specs=[pl.BlockSpec((tm, D_MODEL), lambda i: (i, 0)),
                  pl.BlockSpec((tm, TOP_K * D_MODEL), lambda i: (i, 0)),
                  pl.BlockSpec((None, 1, D_MODEL), lambda i: (_cond_of_rows(i * tm) * 6 + 5, 0, 0))],
        out_specs=pl.BlockSpec((tm, D_MODEL), lambda i: (i, 0)),
        compiler_params=_params(1), name="moe_combine",
    )(x, y4, mod)


def _reorder_w_in(w_in):
    pad = jnp.zeros((DEPTH, D_MODEL, N_IN_PAD - OFF_GATE - 32), w_in.dtype)
    return jnp.concatenate([w_in[:, :, :4096], w_in[:, :, 4128:], w_in[:, :, 4096:4128], pad], axis=-1).astype(bf16)


def kernel(x_prompt, x_sample, cache_attn_k, cache_attn_v, state_mlstm_C, state_mlstm_n, state_mlstm_m, c, c_ctx, ada_w, ada_b, norm1_g, norm2_g, w_in, ml_b_i, ml_b_f, ml_norm_g, da_lam, da_subln_g, hy_conv_w, hy_conv_b, hy_f_w1, hy_f_b1, hy_f_freq, hy_f_w2, hy_f_b2, hy_f_w3, hy_d, w_br_ml, w_br_da, w_br_hy, b_merge, w_out, router_w, router_b, moe_w_gate, moe_b_gate, moe_w_up, moe_b_up, moe_w_down, moe_b_down, final_g):
    x = jnp.concatenate([x_prompt.reshape(T_CTX, D_MODEL), x_sample.reshape(T_LAT, D_MODEL)], axis=0)
    cond = jnp.concatenate([c_ctx[None], c, jnp.zeros((COND_PAD - N_COND, D_MODEL), f32)], axis=0)
    mod_all = ada_modulation(cond, ada_w, ada_b).reshape(DEPTH, COND_PAD * 6, 1, D_MODEL)
    w_in_p = _reorder_w_in(w_in)

    cos_t, sin_t = rope_tables(DEC_SEQ)
    dft = {L: dft_matrices(L) for L in (SEQ, DEC_SEQ)}
    cache_k = cache_attn_k.reshape(DEC_BATCH, DEPTH, PAST_LEN, DA_HEADS * 2 * DA_DH)
    cache_v = cache_attn_v.reshape(DEC_BATCH, DEPTH, PAST_LEN, DA_HEADS * DA_VD)
    lat_blk0 = T_CTX // DEC_SEQ

    new_k, new_v, new_C, new_n, new_m = [], [], [], [], []
    for l in range(DEPTH):
        lam_init = 0.8 - 0.6 * math.exp(-0.3 * l)
        mod = mod_all[l]
        h = norm_mod(x, norm1_g[l], mod, 1, 0)
        u = in_projection(h, w_in_p, l)
        new_k.append(u[:T_CTX, OFF_DK:OFF_DK + 1024].reshape(BATCH, SEQ, DA_HEADS, 2, DA_DH))
        new_v.append(u[:T_CTX, OFF_DV:OFF_DV + 1024].reshape(BATCH, SEQ, DA_HEADS, DA_VD))

        g = u[:, OFF_GATE:OFF_GATE + 32].reshape(T_ALL, 4, ML_HEADS)
        gcol = jnp.transpose(g, (2, 0, 1))
        grow = jnp.transpose(g, (2, 1, 0))
        gb = jnp.concatenate([ml_b_i[l], ml_b_f[l]], axis=0)
        bcol = gb.T.reshape(ML_HEADS, 1, 4)
        brow = gb.T.reshape(ML_HEADS, 4, 1)
        ng = ml_norm_g[l].reshape(ML_HEADS, 1, ML_DV)
        hml_c, C_c, n_c, m_c = mlstm(u, gcol, grow, bcol, brow, ng, nseq=BATCH, L=SEQ, row_blk0=0)
        init = (state_mlstm_C[:, l], state_mlstm_n[:, l].reshape(DEC_BATCH, 2, ML_HEADS, 1, ML_DK),
                jnp.broadcast_to(state_mlstm_m[:, l][..., None, None], (DEC_BATCH, 2, ML_HEADS, 1, LANE)))
        hml_l, _, _, _ = mlstm(u, gcol, grow, bcol, brow, ng, nseq=DEC_BATCH, L=DEC_SEQ, row_blk0=lat_blk0,
                               init=init)
        h_ml = jnp.concatenate([hml_c, hml_l], axis=0)
        new_C.append(jnp.transpose(C_c, (0, 2, 1, 3, 4)))
        new_n.append(jnp.transpose(n_c, (0, 2, 1, 3)))
        new_m.append(jnp.transpose(m_c[..., 0], (0, 2, 1)))

        hda_c = diff_attention(u, da_lam, da_subln_g, lam_init, layer=l, nseq=BATCH, L=SEQ, row_blk0=0, tq=SEQ)
        hda_l = diff_attention(u, da_lam, da_subln_g, lam_init, layer=l, nseq=DEC_BATCH, L=DEC_SEQ,
                               row_blk0=lat_blk0, tq=256, cache=(cache_k, cache_v), rope=(cos_t, sin_t))
        h_da = jnp.concatenate([hda_c, hda_l], axis=0)

        pad_f = lambda a: jnp.pad(a, ((0, LANE - a.shape[0]), (0, LANE - a.shape[1])))
        w1 = pad_f(hy_f_w1[l])
        w2 = pad_f(hy_f_w2[l])
        w3 = jnp.pad(hy_f_w3[l], ((0, LANE - HY_FH), (0, 0)))
        b1 = jnp.pad(hy_f_b1[l], (0, LANE - HY_FH))[None]
        b2 = jnp.pad(hy_f_b2[l], (0, LANE - HY_FH))[None]
        fq = jnp.pad(hy_f_freq[l], ((0, 0), (0, LANE - HY_FH)))
        conv_w = jnp.transpose(hy_conv_w[l].reshape(3, 3, HY_WIDTH), (1, 0, 2))
        conv_b = hy_conv_b[l].reshape(3, 1, HY_WIDTH)
        h_hy_parts = []
        for (nseq, L, blk0) in ((BATCH, SEQ, 0), (DEC_BATCH, DEC_SEQ, lat_blk0)):
            fwd, inv = dft[L]
            hf = hyena_filter_spectrum(L, fwd, w1, b1, fq, w2, b2, w3)
            hf = jnp.transpose(hf.reshape(fwd.shape[0], 2, HY_WIDTH), (1, 0, 2))
            h_hy_parts.append(hyena(u, conv_w, conv_b, hf, hy_d[l], fwd, inv, nseq=nseq, L=L, row_blk0=blk0))
        h_hy = jnp.concatenate(h_hy_parts, axis=0)

        merged = merge_branches(h_ml, h_da, h_hy, w_br_ml, w_br_da, w_br_hy, u, b_merge, l)
        x = out_projection(merged, w_out, x, mod, l)

        h2, top_i, top_w = norm_router(x, norm2_g[l], mod, router_w[l], router_b[l])
        dest, tok_of_slot, tile_expert, n_used = route(top_i[:, :TOP_K])
        xs = jnp.take(h2, tok_of_slot, axis=0)
        gate_sorted = jnp.zeros((MOE_ROWS,), f32).at[dest].set(top_w[:, :TOP_K].reshape(-1)).reshape(MOE_ROWS, 1)
        act = expert_up(xs, moe_w_gate, moe_b_gate, moe_w_up, moe_b_up, tile_expert, n_used, l)
        ys = expert_down(act, moe_w_down, moe_b_down, gate_sorted, tile_expert, n_used, l)
        y4 = jnp.take(ys, dest, axis=0).reshape(T_ALL, TOP_K * D_MODEL)
        x = moe_combine(x, y4, mod)

    y = final_norm(x, final_g)
    y_prompt = y[:T_CTX].reshape(BATCH, SEQ, D_MODEL)
    y_sample = y[T_CTX:].reshape(DEC_BATCH, DEC_SEQ, D_MODEL)
    return (y_prompt, y_sample, jnp.stack(new_k, axis=1), jnp.stack(new_v, axis=1),
            jnp.stack(new_C, axis=1), jnp.stack(new_n, axis=1), jnp.stack(new_m, axis=1))
```

```python
import functools
import math

import jax
import jax.numpy as jnp
from jax import lax
from jax.experimental import pallas as pl
from jax.experimental.pallas import tpu as pltpu

f32 = jnp.float32
bf16 = jnp.bfloat16

D_MODEL = 4096
BATCH = 32
SEQ = 256
DEPTH = 2
DEC_BATCH = 2
DEC_SEQ = 1024
PAST_LEN = 256
GRID_W = 64
EPS = 1e-6
ML_HEADS = 8
ML_DK = 128
ML_DV = 128
DA_HEADS = 8
DA_DH = 64
DA_VD = 128
ROPE_BASE = 10000.0
HY_WIDTH = 1024
HY_BANDS = 8
HY_EMB = 1 + 2 * HY_BANDS
HY_FH = 64
HY_TARGET = 1e-2
HY_MIN_DECAY = math.log(HY_TARGET) / 1.5
HY_MAX_DECAY = math.log(HY_TARGET) / 0.3
N_BRANCH = 3
N_EXPERTS = 32
TOP_K = 4
D_EXPERT = D_MODEL // 4
SWIGLU_ALPHA = 1.702
SWIGLU_LIMIT = 7.0

T_CTX = BATCH * SEQ
T_LAT = DEC_BATCH * DEC_SEQ
T_ALL = T_CTX + T_LAT
N_COND = 1 + DEC_BATCH
COND_PAD = 8

W_IN_ML = (0, 4096)
W_IN_GATE = (4096, 128)
W_IN_MIX = (4128, 6144)
W_IN_GL = (10272, 12288)
N_IN = 22560
OFF_MQ, OFF_MK, OFF_MV, OFF_MO = 0, 1024, 2048, 3072
OFF_DQ, OFF_DK, OFF_DV = 0, 1024, 2048
OFF_HU = 3072

LANE = 128
VMEM_LIMIT = 56 * 1024 * 1024

ML_CHUNK = 256
MOE_TILE = 512
MOE_NT = (T_ALL * TOP_K) // MOE_TILE + N_EXPERTS
MOE_ROWS = MOE_NT * MOE_TILE


def _params(n_axes):
    return pltpu.CompilerParams(dimension_semantics=("arbitrary",) * n_axes,
                                vmem_limit_bytes=VMEM_LIMIT)


def _cond_of_rows(row0):
    return jnp.where(row0 < T_CTX, 0, 1 + (row0 - T_CTX) // DEC_SEQ)


def _ada_kernel(c_ref, w_ref, b_ref, o_ref):
    c = c_ref[...]
    s = c * jax.nn.sigmoid(c)
    o_ref[...] = jnp.dot(s.astype(bf16), w_ref[...].astype(bf16),
                         preferred_element_type=f32) + b_ref[...]


def ada_modulation(cond, ada_w, ada_b):
    tn = 1024
    n = 6 * D_MODEL
    return pl.pallas_call(
        _ada_kernel,
        out_shape=jax.ShapeDtypeStruct((DEPTH, COND_PAD, n), f32),
        grid=(DEPTH, n // tn),
        in_specs=[pl.BlockSpec((COND_PAD, D_MODEL), lambda l, j: (0, 0)),
                  pl.BlockSpec((None, D_MODEL, tn), lambda l, j: (l, 0, j)),
                  pl.BlockSpec((None, 1, tn), lambda l, j: (l, 0, j))],
        out_specs=pl.BlockSpec((None, COND_PAD, tn), lambda l, j: (l, 0, j)),
        compiler_params=_params(2), name="ada_mod",
    )(cond, ada_w, ada_b.reshape(DEPTH, 1, n))


def _norm_mod_kernel(x_ref, g_ref, sc_ref, sh_ref, o_ref):
    x = x_ref[...]
    y = x * lax.rsqrt(jnp.mean(x * x, axis=-1, keepdims=True) + EPS) * g_ref[...]
    o_ref[...] = (y * (1.0 + sc_ref[...]) + sh_ref[...]).astype(o_ref.dtype)


def norm_mod(x, g, mod, j_scale, j_shift):
    tm = 512
    return pl.pallas_call(
        _norm_mod_kernel,
        out_shape=jax.ShapeDtypeStruct((T_ALL, D_MODEL), bf16),
        grid=(T_ALL // tm,),
        in_specs=[pl.BlockSpec((tm, D_MODEL), lambda i: (i, 0)),
                  pl.BlockSpec((1, D_MODEL), lambda i: (0, 0)),
                  pl.BlockSpec((None, 1, D_MODEL), lambda i: (_cond_of_rows(i * tm) * 6 + j_scale, 0, 0)),
                  pl.BlockSpec((None, 1, D_MODEL), lambda i: (_cond_of_rows(i * tm) * 6 + j_shift, 0, 0))],
        out_specs=pl.BlockSpec((tm, D_MODEL), lambda i: (i, 0)),
        compiler_params=_params(1), name="norm_mod",
    )(x, g.reshape(1, D_MODEL), mod, mod)


def _final_norm_kernel(x_ref, g_ref, o_ref):
    x = x_ref[...]
    o_ref[...] = x * lax.rsqrt(jnp.mean(x * x, axis=-1, keepdims=True) + EPS) * g_ref[...]


def final_norm(x, g, row0, nrows):
    tm = 512
    return pl.pallas_call(
        _final_norm_kernel,
        out_shape=jax.ShapeDtypeStruct((nrows, D_MODEL), f32),
        grid=(nrows // tm,),
        in_specs=[pl.BlockSpec((tm, D_MODEL), lambda i: (row0 // tm + i, 0)),
                  pl.BlockSpec((1, D_MODEL), lambda i: (0, 0))],
        out_specs=pl.BlockSpec((tm, D_MODEL), lambda i: (i, 0)),
        compiler_params=_params(1), name="final_norm",
    )(x, g.reshape(1, D_MODEL))


def _in_proj_kernel(*refs, shift):
    if shift:
        a_ref, lo_ref, hi_ref, o_ref = refs
        w = jnp.concatenate([lo_ref[:, shift:], hi_ref[:, :shift]], axis=1)
    else:
        a_ref, lo_ref, o_ref = refs
        w = lo_ref[...]
    o_ref[...] = jnp.dot(a_ref[...], w.astype(bf16), preferred_element_type=f32).astype(o_ref.dtype)


def in_projection(h, w_in, layer, cols, out_dtype, tn=256):
    start, width = cols
    shift = start % LANE
    base = start - shift
    assert base % tn == 0 and width % tn == 0
    tm = 2048
    in_specs = [pl.BlockSpec((tm, D_MODEL), lambda i, j: (i, 0), pipeline_mode=pl.Buffered(1)),
                pl.BlockSpec((None, D_MODEL, tn), lambda i, j: (layer, 0, base // tn + j))]
    args = [h, w_in]
    if shift:
        in_specs.append(pl.BlockSpec((None, D_MODEL, LANE),
                                     lambda i, j: (layer, 0, (base + tn) // LANE + j * (tn // LANE))))
        args.append(w_in)
    return pl.pallas_call(
        functools.partial(_in_proj_kernel, shift=shift),
        out_shape=jax.ShapeDtypeStruct((T_ALL, width), out_dtype),
        grid=(T_ALL // tm, width // tn),
        in_specs=in_specs,
        out_specs=pl.BlockSpec((tm, tn), lambda i, j: (i, j)),
        compiler_params=_params(2), name="in_proj",
    )(*args)


def _log_sigmoid(x):
    return jnp.minimum(x, 0.0) - jnp.log(1.0 + jnp.exp(-jnp.abs(x)))


def _mlstm_chunk(q, k, v, i_col, f_col, i_row, f_row, C, n, m, mask_self, mask_other):
    b_col = jnp.sum(jnp.where(mask_self, f_row, 0.0), axis=1, keepdims=True)
    b_row = jnp.sum(jnp.where(mask_other, f_col, 0.0), axis=0, keepdims=True)
    total = jnp.sum(f_col, axis=0, keepdims=True)
    log_d = jnp.where(mask_self, b_col - b_row + i_row, -jnp.inf)
    m_inter = b_col + m
    m_s = jnp.maximum(m_inter, jnp.max(log_d, axis=1, keepdims=True))
    qb = q.astype(bf16)
    kb = k.astype(bf16)
    vb = v.astype(bf16)
    s = lax.dot_general(qb, kb, (((1,), (1,)), ((), ())), preferred_element_type=f32)
    w_intra = s * jnp.exp(log_d - m_s)
    w_inter = jnp.exp(m_inter - m_s)
    num = (jnp.dot(w_intra.astype(bf16), vb, preferred_element_type=f32)
           + w_inter * jnp.dot(qb, C.astype(bf16), preferred_element_type=f32))
    den = jnp.sum(w_intra, axis=1, keepdims=True) + w_inter * jnp.sum(q * n, axis=1, keepdims=True)
    h = num / jnp.maximum(jnp.abs(den), jnp.exp(-m_s))
    log_w = total - b_col + i_col
    m_new = jnp.maximum(total + m, jnp.max(log_w, axis=0, keepdims=True))
    w = jnp.exp(log_w - m_new)
    carry = jnp.exp(total + m - m_new)
    kw = k * w
    C_new = carry * C + lax.dot_general(kw.astype(bf16), vb, (((0,), (0,)), ((), ())),
                                        preferred_element_type=f32)
    n_new = carry * n + jnp.sum(kw, axis=0, keepdims=True)
    return h, C_new, n_new, m_new


def _mlstm_kernel(*refs, nchunks, zero_init):
    if zero_init:
        (q_ref, k_ref, v_ref, o_ref, gc_ref, gr_ref, bc_ref, br_ref, ng_ref,
         h_out, C_out, n_out, m_out, hacc) = refs
    else:
        (q_ref, k_ref, v_ref, o_ref, gc_ref, gr_ref, bc_ref, br_ref, ng_ref, C0_ref, n0_ref, m0_ref,
         h_out, C_out, n_out, m_out, hacc) = refs
    CH = ML_CHUNK
    gc = gc_ref[...] + bc_ref[...]
    gr = gr_ref[...] + br_ref[...]
    row = lax.broadcasted_iota(jnp.int32, (CH, CH), 0)
    col = lax.broadcasted_iota(jnp.int32, (CH, CH), 1)
    lower = col <= row
    upper = col >= row
    scale = ML_DK ** -0.5
    for d in range(2):
        if zero_init:
            C = jnp.zeros((ML_DK, ML_DV), f32)
            n = jnp.zeros((1, ML_DK), f32)
            m = jnp.zeros((1, 1), f32)
        else:
            C = C0_ref[d]
            n = n0_ref[d]
            m = m0_ref[d][:, 0:1]
        mask_self, mask_other = (lower, upper) if d == 0 else (upper, lower)
        order = range(nchunks) if d == 0 else range(nchunks - 1, -1, -1)
        for c in order:
            sl = slice(c * CH, (c + 1) * CH)
            q = q_ref[sl, :]
            k = k_ref[sl, :] * scale
            v = v_ref[sl, :]
            i_col = gc[sl, d:d + 1]
            f_col = _log_sigmoid(gc[sl, 2 + d:3 + d])
            i_row = gr[d:d + 1, sl]
            f_row = _log_sigmoid(gr[2 + d:3 + d, sl])
            h, C, n, m = _mlstm_chunk(q, k, v, i_col, f_col, i_row, f_row, C, n, m, mask_self, mask_other)
            if d == 0:
                hacc[sl, :] = h
            else:
                hacc[sl, :] += h
        C_out[d] = C
        n_out[d:d + 1, :] = n
        m_out[d:d + 1, :] = jnp.broadcast_to(m, (1, LANE))
    hg = jax.nn.sigmoid(o_ref[...]) * hacc[...]
    y = hg * lax.rsqrt(jnp.mean(hg * hg, axis=-1, keepdims=True) + EPS) * ng_ref[...]
    h_out[...] = y.astype(h_out.dtype)


def mlstm(u, gcol, grow, bcol, brow, norm_g, *, nseq, L, row_blk0, init=None):
    H = ML_HEADS
    nchunks = L // ML_CHUNK
    cb = lambda off: off // LANE
    in_specs = [
        pl.BlockSpec((L, LANE), lambda b, h: (row_blk0 + b, cb(OFF_MQ) + h)),
        pl.BlockSpec((L, LANE), lambda b, h: (row_blk0 + b, cb(OFF_MK) + h)),
        pl.BlockSpec((L, LANE), lambda b, h: (row_blk0 + b, cb(OFF_MV) + h)),
        pl.BlockSpec((L, LANE), lambda b, h: (row_blk0 + b, cb(OFF_MO) + h)),
        pl.BlockSpec((None, L, 4), lambda b, h: (h, row_blk0 + b, 0)),
        pl.BlockSpec((None, 4, L), lambda b, h: (h, 0, row_blk0 + b)),
        pl.BlockSpec((None, 1, 4), lambda b, h: (h, 0, 0)),
        pl.BlockSpec((None, 4, 1), lambda b, h: (h, 0, 0)),
        pl.BlockSpec((None, 1, LANE), lambda b, h: (h, 0, 0)),
    ]
    args = [u, u, u, u, gcol, grow, bcol, brow, norm_g]
    if init is not None:
        C0, n0, m0 = init
        in_specs += [
            pl.BlockSpec((None, 2, None, ML_DK, ML_DV), lambda b, h: (b, 0, h, 0, 0)),
            pl.BlockSpec((None, 2, None, 1, ML_DK), lambda b, h: (b, 0, h, 0, 0)),
            pl.BlockSpec((None, 2, None, 1, LANE), lambda b, h: (b, 0, h, 0, 0)),
        ]
        args += [C0, n0, m0]
    out_shape = (jax.ShapeDtypeStruct((nseq * L, ML_HEADS * ML_DV), bf16),
                 jax.ShapeDtypeStruct((nseq, H, 2, ML_DK, ML_DV), f32),
                 jax.ShapeDtypeStruct((nseq, H, 2, ML_DK), f32),
                 jax.ShapeDtypeStruct((nseq, H, 2, LANE), f32))
    out_specs = (pl.BlockSpec((L, LANE), lambda b, h: (b, h)),
                 pl.BlockSpec((None, None, 2, ML_DK, ML_DV), lambda b, h: (b, h, 0, 0, 0)),
                 pl.BlockSpec((None, None, 2, ML_DK), lambda b, h: (b, h, 0, 0)),
                 pl.BlockSpec((None, None, 2, LANE), lambda b, h: (b, h, 0, 0)))
    return pl.pallas_call(
        functools.partial(_mlstm_kernel, nchunks=nchunks, zero_init=init is None),
        out_shape=out_shape, grid=(nseq, H), in_specs=in_specs, out_specs=out_specs,
        scratch_shapes=[pltpu.VMEM((L, ML_DV), f32)],
        compiler_params=_params(2), name="mlstm_ctx" if init is None else "mlstm_lat",
    )(*args)


def _rope(x, cos, sin_signed):
    lane = lax.broadcasted_iota(jnp.int32, x.shape, 1)
    partner = jnp.where((lane & 16) == 0, pltpu.roll(x, LANE - 16, axis=1), pltpu.roll(x, 16, axis=1))
    return x * cos + partner * sin_signed


def _attn_kernel(*refs, n_cache, lam_init):
    if n_cache:
        (q_ref, k_ref, v_ref, lam_ref, g_ref, ck_ref, cv_ref, cq_ref, sq_ref, ckk_ref, skk_ref,
         o_ref, k_all, v_all) = refs
    else:
        q_ref, k_ref, v_ref, lam_ref, g_ref, o_ref, k_all, v_all = refs

    @pl.when(pl.program_id(2) == 0)
    def _():
        if n_cache:
            k_all[0:n_cache, :] = ck_ref[...].astype(bf16)
            v_all[0:n_cache, :] = cv_ref[...].astype(bf16)
            k_all[n_cache:, :] = _rope(k_ref[...], ckk_ref[...], skk_ref[...]).astype(bf16)
        else:
            k_all[...] = k_ref[...].astype(bf16)
        v_all[n_cache:, :] = v_ref[...].astype(bf16)

    q = q_ref[...]
    if n_cache:
        q = _rope(q, cq_ref[...], sq_ref[...])
    q = q * (DA_DH ** -0.5)
    lane = lax.broadcasted_iota(jnp.int32, q.shape, 1)
    lp = lam_ref[...]
    lam = (jnp.exp(jnp.sum(lp[0:1] * lp[1:2], axis=1, keepdims=True))
           - jnp.exp(jnp.sum(lp[2:3] * lp[3:4], axis=1, keepdims=True)) + lam_init)
    kb = k_all[...]
    probs = []
    for c in range(2):
        qc = jnp.where((lane >= c * DA_DH) & (lane < (c + 1) * DA_DH), q, 0.0).astype(bf16)
        s = lax.dot_general(qc, kb, (((1,), (1,)), ((), ())), preferred_element_type=f32)
        p = jnp.exp(s - jnp.max(s, axis=1, keepdims=True))
        probs.append(p / jnp.sum(p, axis=1, keepdims=True))
    a = probs[0] - lam * probs[1]
    o = jnp.dot(a.astype(bf16), v_all[...], preferred_element_type=f32)
    y = o * lax.rsqrt(jnp.mean(o * o, axis=-1, keepdims=True) + EPS) * g_ref[...]
    o_ref[...] = (y * (1.0 - lam_init)).astype(o_ref.dtype)


def diff_attention(u, da_lam, subln_g, lam_init, *, layer, nseq, L, row_blk0, tq, cache=None, rope=None):
    nq = L // tq
    cb = lambda off: off // LANE
    n_cache = 0 if cache is None else PAST_LEN
    in_specs = [
        pl.BlockSpec((tq, LANE), lambda b, h, i: ((row_blk0 + b) * nq + i, cb(OFF_DQ) + h)),
        pl.BlockSpec((L, LANE), lambda b, h, i: (row_blk0 + b, cb(OFF_DK) + h)),
        pl.BlockSpec((L, LANE), lambda b, h, i: (row_blk0 + b, cb(OFF_DV) + h)),
        pl.BlockSpec((None, 4, DA_DH), lambda b, h, i: (layer, 0, 0)),
        pl.BlockSpec((None, 1, DA_VD), lambda b, h, i: (layer, 0, 0)),
    ]
    args = [u, u, u, da_lam, subln_g.reshape(DEPTH, 1, DA_VD)]
    if cache is not None:
        ck, cv = cache
        cos, sin = rope
        in_specs += [
            pl.BlockSpec((None, None, PAST_LEN, LANE), lambda b, h, i: (b, layer, 0, h)),
            pl.BlockSpec((None, None, PAST_LEN, LANE), lambda b, h, i: (b, layer, 0, h)),
            pl.BlockSpec((tq, LANE), lambda b, h, i: (i, 0)),
            pl.BlockSpec((tq, LANE), lambda b, h, i: (i, 0)),
            pl.BlockSpec((L, LANE), lambda b, h, i: (0, 0)),
            pl.BlockSpec((L, LANE), lambda b, h, i: (0, 0)),
        ]
        args += [ck, cv, cos, sin, cos, sin]
    return pl.pallas_call(
        functools.partial(_attn_kernel, n_cache=n_cache, lam_init=lam_init),
        out_shape=jax.ShapeDtypeStruct((nseq * L, DA_HEADS * DA_VD), bf16),
        grid=(nseq, DA_HEADS, nq), in_specs=in_specs,
        out_specs=pl.BlockSpec((tq, LANE), lambda b, h, i: (b * nq + i, h)),
        scratch_shapes=[pltpu.VMEM((n_cache + L, LANE), bf16), pltpu.VMEM((n_cache + L, LANE), bf16)],
        compiler_params=_params(3), name="attn_ctx" if cache is None else "attn_lat",
    )(*args)


def rope_tables(L):
    pos = jnp.arange(L)
    row = (pos // GRID_W).astype(f32)
    colp = (pos % GRID_W).astype(f32)
    lane = jnp.arange(LANE)
    d = lane % DA_DH
    half = DA_DH // 4
    inv = ROPE_BASE ** (-(d % half).astype(f32) / half)
    p = jnp.where((d < DA_DH // 2)[None, :], row[:, None], colp[:, None])
    ang = p * inv[None, :]
    sign = jnp.where((d % (2 * half)) < half, -1.0, 1.0)
    return jnp.cos(ang), jnp.sin(ang) * sign[None, :]


def dft_matrices(L):
    n = 2 * L
    nf = L + 1
    nfp = L + 16
    k = jnp.arange(nfp)
    valid = (k < nf)
    s = jnp.arange(L)
    ang_f = (2.0 * math.pi / n) * ((k[:, None] * s[None, :]) % n).astype(f32)
    fr = jnp.where(valid[:, None], jnp.cos(ang_f), 0.0)
    fi = jnp.where(valid[:, None], -jnp.sin(ang_f), 0.0)
    fwd = jnp.concatenate([fr, fi], axis=0)
    t = jnp.arange(L) + L // 2
    ang_i = (2.0 * math.pi / n) * ((t[:, None] * k[None, :]) % n).astype(f32)
    wk = jnp.where((k == 0) | (k == L), 1.0, 2.0) / n
    wk = jnp.where(valid, wk, 0.0)
    inv = jnp.concatenate([jnp.cos(ang_i) * wk[None, :], -jnp.sin(ang_i) * wk[None, :]], axis=1)
    return fwd, inv


def _hy_filter_kernel(feat_ref, w1_ref, b1_ref, fq_ref, w2_ref, b2_ref, w3_ref, off_ref, dl_ref, fwd_ref, o_ref):
    hp = lax.Precision.HIGHEST
    fq = fq_ref[...]
    hdn = jnp.sin(fq[0:1] * (jnp.dot(feat_ref[...], w1_ref[...], precision=hp, preferred_element_type=f32)
                             + b1_ref[...]))
    hdn = jnp.sin(fq[1:2] * (jnp.dot(hdn, w2_ref[...], precision=hp, preferred_element_type=f32) + b2_ref[...]))
    filt = jnp.dot(hdn, w3_ref[...], precision=hp, preferred_element_type=f32)
    filt = filt * jnp.exp(-off_ref[...] * dl_ref[...])
    o_ref[...] = jnp.dot(fwd_ref[...], filt, precision=hp, preferred_element_type=f32)


def hyena_filter_spectrum(L, fwd, w1, b1, fq, w2, b2, w3):
    nfp2 = fwd.shape[0]
    pos = jnp.arange(L, dtype=f32)
    t = pos / (L - 1)
    ang = (2.0 * math.pi * pos / L)[:, None] * jnp.linspace(1e-4, HY_BANDS - 1, HY_BANDS, dtype=f32)
    feats = jnp.concatenate([t[:, None], jnp.cos(ang), -jnp.sin(ang)], axis=-1)
    feats = jnp.pad(feats, ((0, 0), (0, LANE - HY_EMB)))
    offset = (jnp.abs(pos - L // 2) / (L // 2))[:, None]
    deltas = jnp.abs(jnp.linspace(HY_MIN_DECAY, HY_MAX_DECAY, HY_WIDTH, dtype=f32))
    deltas = jnp.concatenate([deltas, deltas])[None, :]
    cbw = 256
    nw = 2 * HY_WIDTH
    full = lambda shape: pl.BlockSpec(shape, lambda j: (0,) * len(shape))
    return pl.pallas_call(
        _hy_filter_kernel,
        out_shape=jax.ShapeDtypeStruct((nfp2, nw), f32),
        grid=(nw // cbw,),
        in_specs=[full((L, LANE)), full((LANE, LANE)), full((1, LANE)), full((2, LANE)), full((LANE, LANE)),
                  full((1, LANE)), pl.BlockSpec((LANE, cbw), lambda j: (0, j)), full((L, 1)),
                  pl.BlockSpec((1, cbw), lambda j: (0, j)), full((nfp2, L))],
        out_specs=pl.BlockSpec((nfp2, cbw), lambda j: (0, j)),
        compiler_params=_params(1), name="hyena_filter",
    )(feats, w1, b1, fq, w2, b2, w3, offset, deltas, fwd)


def _hyena_kernel(v_ref, x1_ref, x2_ref, cw_ref, cbias_ref, hf_ref, d_ref, fwd_ref, inv_ref, o_ref, *, L):
    nfp = fwd_ref.shape[0] // 2
    cb = v_ref.shape[1]
    rows = lax.broadcasted_iota(jnp.int32, (L, cb), 0)

    def conv3(x_ref, part):
        x = x_ref[...]
        w = cw_ref[part]
        prev = jnp.where(rows == 0, 0.0, pltpu.roll(x, 1, axis=0))
        nxt = jnp.where(rows == L - 1, 0.0, pltpu.roll(x, L - 1, axis=0))
        return w[0:1] * prev + w[1:2] * x + w[2:3] * nxt + cbias_ref[part]

    z = conv3(v_ref, 0)
    gates = (conv3(x1_ref, 1), conv3(x2_ref, 2))
    fwd = fwd_ref[...]
    inv = inv_ref[...]
    dsk = d_ref[...]
    for o in range(2):
        zf = jnp.dot(fwd, z.astype(bf16), preferred_element_type=f32)
        zr, zi = zf[:nfp], zf[nfp:]
        hr, hi = hf_ref[o, :nfp, :], hf_ref[o, nfp:, :]
        yf = jnp.concatenate([zr * hr - zi * hi, zr * hi + zi * hr], axis=0).astype(bf16)
        y = jnp.dot(inv, yf, preferred_element_type=f32)
        z = gates[o] * (y + z * dsk[o:o + 1])
    o_ref[...] = z.astype(o_ref.dtype)


def hyena(u, conv_w, conv_b, hf, d_skip, fwd, inv, *, nseq, L, row_blk0):
    cbw = 256
    ncb = HY_WIDTH // cbw
    nfp2 = fwd.shape[0]
    ub = lambda part: (lambda b, j: (row_blk0 + b, (OFF_HU + part * HY_WIDTH) // cbw + j))
    return pl.pallas_call(
        functools.partial(_hyena_kernel, L=L),
        out_shape=jax.ShapeDtypeStruct((nseq * L, HY_WIDTH), bf16),
        grid=(nseq, ncb),
        in_specs=[pl.BlockSpec((L, cbw), ub(0)), pl.BlockSpec((L, cbw), ub(1)), pl.BlockSpec((L, cbw), ub(2)),
                  pl.BlockSpec((3, 3, cbw), lambda b, j: (0, 0, j)),
                  pl.BlockSpec((3, 1, cbw), lambda b, j: (0, 0, j)),
                  pl.BlockSpec((2, nfp2, cbw), lambda b, j: (0, 0, j)),
                  pl.BlockSpec((2, cbw), lambda b, j: (0, j)),
                  pl.BlockSpec((nfp2, L), lambda b, j: (0, 0)),
                  pl.BlockSpec((L, nfp2), lambda b, j: (0, 0))],
        out_specs=pl.BlockSpec((L, cbw), lambda b, j: (b, j)),
        compiler_params=_params(2), name="hyena_ctx" if L == SEQ else "hyena_lat",
    )(u, u, u, conv_w, conv_b, hf, d_skip, fwd.astype(bf16), inv.astype(bf16))


def _merge_kernel(a0, a1, a2, w0, w1, w2, g0, g1, g2, bm_ref, o_ref):
    acc = None
    for j, (a, w, g) in enumerate(((a0, w0, g0), (a1, w1, g1), (a2, w2, g2))):
        gate = jax.nn.sigmoid(g[...].astype(f32) + bm_ref[j])
        term = gate * jnp.dot(a[...], w[...].astype(bf16), preferred_element_type=f32)
        acc = term if acc is None else acc + term
    o_ref[...] = acc.astype(o_ref.dtype)


def merge_branches(h_ml, h_da, h_hy, w_ml, w_da, w_hy, u, b_merge, layer):
    tm, tn = 1024, 512
    a_spec = pl.BlockSpec((tm, 1024), lambda i, j: (i, 0))
    w_spec = pl.BlockSpec((None, 1024, tn), lambda i, j: (layer, 0, j))
    g_spec = lambda br: pl.BlockSpec((tm, tn), lambda i, j: (i, (br * D_MODEL) // tn + j))
    return pl.pallas_call(
        _merge_kernel,
        out_shape=jax.ShapeDtypeStruct((T_ALL, D_MODEL), bf16),
        grid=(T_ALL // tm, D_MODEL // tn),
        in_specs=[a_spec, a_spec, a_spec, w_spec, w_spec, w_spec, g_spec(0), g_spec(1), g_spec(2),
                  pl.BlockSpec((None, N_BRANCH, 1, tn), lambda i, j: (layer, 0, 0, j))],
        out_specs=pl.BlockSpec((tm, tn), lambda i, j: (i, j)),
        compiler_params=_params(2), name="merge",
    )(h_ml, h_da, h_hy, w_ml, w_da, w_hy, u, u, u, b_merge.reshape(DEPTH, N_BRANCH, 1, D_MODEL))


def _out_proj_kernel(a_ref, w_ref, x_ref, g_ref, o_ref):
    o_ref[...] = x_ref[...] + g_ref[...] * jnp.dot(a_ref[...], w_ref[...].astype(bf16),
                                                   preferred_element_type=f32)


def out_projection(merged, w_out, x, mod, layer):
    tm, tn = 1024, 256
    return pl.pallas_call(
        _out_proj_kernel,
        out_shape=jax.ShapeDtypeStruct((T_ALL, D_MODEL), f32),
        grid=(T_ALL // tm, D_MODEL // tn),
        in_specs=[pl.BlockSpec((tm, D_MODEL), lambda i, j: (i, 0)),
                  pl.BlockSpec((None, D_MODEL, tn), lambda i, j: (layer, 0, j)),
                  pl.BlockSpec((tm, tn), lambda i, j: (i, j)),
                  pl.BlockSpec((None, 1, tn), lambda i, j: (_cond_of_rows(i * tm) * 6 + 2, 0, j))],
        out_specs=pl.BlockSpec((tm, tn), lambda i, j: (i, j)),
        compiler_params=_params(2), name="out_proj",
    )(merged, w_out, x, mod)


def _norm_router_kernel(x_ref, g_ref, sc_ref, sh_ref, rw_ref, rb_ref, h_ref, ti_ref, tw_ref):
    x = x_ref[...]
    y = x * lax.rsqrt(jnp.mean(x * x, axis=-1, keepdims=True) + EPS) * g_ref[...]
    h = y * (1.0 + sc_ref[...]) + sh_ref[...]
    h_ref[...] = h.astype(h_ref.dtype)
    logits = jnp.dot(h, rw_ref[...], precision=lax.Precision.HIGHEST, preferred_element_type=f32) + rb_ref[...]
    lane = lax.broadcasted_iota(jnp.int32, logits.shape, 1)
    logits = jnp.where(lane < N_EXPERTS, logits, -jnp.inf)
    idx_out = jnp.zeros(logits.shape, jnp.int32)
    val_out = jnp.zeros(logits.shape, f32)
    top = None
    for k in range(TOP_K):
        mx = jnp.max(logits, axis=1, keepdims=True)
        idx = jnp.min(jnp.where(logits == mx, lane, LANE), axis=1, keepdims=True)
        if top is None:
            top = mx
        idx_out = jnp.where(lane == k, idx, idx_out)
        val_out = jnp.where(lane == k, jnp.exp(mx - top), val_out)
        logits = jnp.where(lane == idx, -jnp.inf, logits)
    ti_ref[...] = idx_out
    tw_ref[...] = val_out / jnp.sum(val_out, axis=1, keepdims=True)


def norm_router(x, g, mod, router_w, router_b):
    tm = 256
    rw = jnp.pad(router_w, ((0, 0), (0, LANE - N_EXPERTS)))
    rb = jnp.pad(router_b, (0, LANE - N_EXPERTS)).reshape(1, LANE)
    row = pl.BlockSpec((tm, LANE), lambda i: (i, 0))
    return pl.pallas_call(
        _norm_router_kernel,
        out_shape=(jax.ShapeDtypeStruct((T_ALL, D_MODEL), f32),
                   jax.ShapeDtypeStruct((T_ALL, LANE), jnp.int32),
                   jax.ShapeDtypeStruct((T_ALL, LANE), f32)),
        grid=(T_ALL // tm,),
        in_specs=[pl.BlockSpec((tm, D_MODEL), lambda i: (i, 0)),
                  pl.BlockSpec((1, D_MODEL), lambda i: (0, 0)),
                  pl.BlockSpec((None, 1, D_MODEL), lambda i: (_cond_of_rows(i * tm) * 6 + 4, 0, 0)),
                  pl.BlockSpec((None, 1, D_MODEL), lambda i: (_cond_of_rows(i * tm) * 6 + 3, 0, 0)),
                  pl.BlockSpec((D_MODEL, LANE), lambda i: (0, 0)),
                  pl.BlockSpec((1, LANE), lambda i: (0, 0))],
        out_specs=(pl.BlockSpec((tm, D_MODEL), lambda i: (i, 0)), row, row),
        compiler_params=_params(1), name="norm_router",
    )(x, g.reshape(1, D_MODEL), mod, mod, rw, rb)


def route(top_i):
    eid = top_i.reshape(-1)
    onehot = (eid[:, None] == jnp.arange(N_EXPERTS)[None, :]).astype(jnp.int32)
    rank = jnp.take_along_axis(jnp.cumsum(onehot, axis=0) - onehot, eid[:, None], axis=1)[:, 0]
    counts = jnp.sum(onehot, axis=0)
    padded = ((counts + MOE_TILE - 1) // MOE_TILE) * MOE_TILE
    ends = jnp.cumsum(padded)
    dest = (ends - padded)[eid] + rank
    tok_of_slot = jnp.zeros((MOE_ROWS,), jnp.int32).at[dest].set(
        jnp.arange(eid.shape[0], dtype=jnp.int32) // TOP_K, mode="promise_in_bounds", unique_indices=True)
    tile_start = jnp.arange(MOE_NT, dtype=jnp.int32) * MOE_TILE
    tile_expert = jnp.minimum(jnp.sum((ends[None, :] <= tile_start[:, None]).astype(jnp.int32), axis=1),
                              N_EXPERTS - 1)
    n_used = (ends[-1] // MOE_TILE).astype(jnp.int32).reshape(1)
    return dest.astype(jnp.int32), tok_of_slot, tile_expert, n_used


def _gather_rows_kernel(nu_ref, idx_ref, src_hbm, o_ref, buf, sem):
    n = buf.shape[0]

    @pl.when(pl.program_id(0) < nu_ref[0])
    def _():
        def issue(i, carry):
            pltpu.make_async_copy(src_hbm.at[pl.ds(idx_ref[0, i], 1)], buf.at[pl.ds(i, 1)], sem).start()
            return carry

        lax.fori_loop(0, n, issue, 0, unroll=8)
        pltpu.make_async_copy(src_hbm.at[pl.ds(0, n)], buf, sem).wait()
        o_ref[...] = buf[...].astype(o_ref.dtype)

    @pl.when(pl.program_id(0) >= nu_ref[0])
    def _():
        o_ref[...] = jnp.zeros(o_ref.shape, o_ref.dtype)


def gather_token_rows(h, tok_of_slot, n_used):
    tile = lambda i, nu: jnp.minimum(i, nu[0] - 1)
    return pl.pallas_call(
        _gather_rows_kernel,
        out_shape=jax.ShapeDtypeStruct((MOE_ROWS, D_MODEL), bf16),
        grid_spec=pltpu.PrefetchScalarGridSpec(
            num_scalar_prefetch=1, grid=(MOE_NT,),
            in_specs=[pl.BlockSpec((None, 1, MOE_TILE), lambda i, nu: (tile(i, nu), 0, 0), memory_space=pltpu.SMEM),
                      pl.BlockSpec(memory_space=pl.ANY)],
            out_specs=pl.BlockSpec((MOE_TILE, D_MODEL), lambda i, nu: (i, 0)),
            scratch_shapes=[pltpu.VMEM((MOE_TILE, D_MODEL), f32), pltpu.SemaphoreType.DMA(())]),
        compiler_params=_params(1), name="moe_gather",
    )(n_used, tok_of_slot.reshape(MOE_NT, 1, MOE_TILE), h)


def _expert_up_kernel(te_ref, nu_ref, x_ref, wg_ref, bg_ref, wu_ref, bu_ref, o_ref):
    @pl.when(pl.program_id(1) < nu_ref[0])
    def _():
        x = x_ref[...]
        a_glu = jnp.dot(x, wg_ref[...].astype(bf16), preferred_element_type=f32) + bg_ref[...]
        a_lin = jnp.dot(x, wu_ref[...].astype(bf16), preferred_element_type=f32) + bu_ref[...]
        a_glu = jnp.minimum(a_glu, SWIGLU_LIMIT)
        a_lin = jnp.clip(a_lin, -SWIGLU_LIMIT, SWIGLU_LIMIT)
        o_ref[...] = (a_glu * jax.nn.sigmoid(SWIGLU_ALPHA * a_glu) * (a_lin + 1.0)).astype(o_ref.dtype)

    @pl.when(pl.program_id(1) >= nu_ref[0])
    def _():
        o_ref[...] = jnp.zeros(o_ref.shape, o_ref.dtype)


def expert_up(xs, w_gate, b_gate, w_up, b_up, tile_expert, n_used, layer):
    tn = 256
    tile = lambda j, r, te, nu: jnp.minimum(r, nu[0] - 1)
    w_spec = pl.BlockSpec((None, None, D_MODEL, tn), lambda j, r, te, nu: (layer, te[tile(j, r, te, nu)], 0, j))
    b_spec = pl.BlockSpec((None, None, 1, tn), lambda j, r, te, nu: (layer, te[tile(j, r, te, nu)], 0, j))
    return pl.pallas_call(
        _expert_up_kernel,
        out_shape=jax.ShapeDtypeStruct((MOE_ROWS, D_EXPERT), bf16),
        grid_spec=pltpu.PrefetchScalarGridSpec(
            num_scalar_prefetch=2, grid=(D_EXPERT // tn, MOE_NT),
            in_specs=[pl.BlockSpec((MOE_TILE, D_MODEL), lambda j, r, te, nu: (tile(j, r, te, nu), 0)),
                      w_spec, b_spec, w_spec, b_spec],
            out_specs=pl.BlockSpec((MOE_TILE, tn), lambda j, r, te, nu: (r, j))),
        compiler_params=_params(2), name="expert_up",
    )(tile_expert, n_used, xs, w_gate, b_gate.reshape(DEPTH, N_EXPERTS, 1, D_EXPERT),
      w_up, b_up.reshape(DEPTH, N_EXPERTS, 1, D_EXPERT))


def _expert_down_kernel(te_ref, nu_ref, a_ref, w_ref, b_ref, o_ref):
    @pl.when(pl.program_id(1) < nu_ref[0])
    def _():
        o_ref[...] = jnp.dot(a_ref[...], w_ref[...].astype(bf16), preferred_element_type=f32) + b_ref[...]

    @pl.when(pl.program_id(1) >= nu_ref[0])
    def _():
        o_ref[...] = jnp.zeros(o_ref.shape, o_ref.dtype)


def expert_down(act, w_down, b_down, tile_expert, n_used, layer):
    tn = 1024
    tile = lambda j, r, te, nu: jnp.minimum(r, nu[0] - 1)
    return pl.pallas_call(
        _expert_down_kernel,
        out_shape=jax.ShapeDtypeStruct((MOE_ROWS, D_MODEL), f32),
        grid_spec=pltpu.PrefetchScalarGridSpec(
            num_scalar_prefetch=2, grid=(D_MODEL // tn, MOE_NT),
            in_specs=[pl.BlockSpec((MOE_TILE, D_EXPERT), lambda j, r, te, nu: (tile(j, r, te, nu), 0)),
                      pl.BlockSpec((None, None, D_EXPERT, tn),
                                   lambda j, r, te, nu: (layer, te[tile(j, r, te, nu)], 0, j)),
                      pl.BlockSpec((None, None, 1, tn), lambda j, r, te, nu: (layer, te[tile(j, r, te, nu)], 0, j))],
            out_specs=pl.BlockSpec((MOE_TILE, tn), lambda j, r, te, nu: (r, j))),
        compiler_params=_params(2), name="expert_down",
    )(tile_expert, n_used, act, w_down, b_down.reshape(DEPTH, N_EXPERTS, 1, D_MODEL))


COMBINE_TM = 128


def _combine_kernel(idx_ref, x_ref, w_ref, g_ref, ys_hbm, o_ref, buf, sem):
    tm = COMBINE_TM
    n = TOP_K * tm

    def issue(i, carry):
        pltpu.make_async_copy(ys_hbm.at[pl.ds(idx_ref[0, i], 1)], buf.at[pl.ds(i, 1)], sem).start()
        return carry

    lax.fori_loop(0, n, issue, 0, unroll=8)
    pltpu.make_async_copy(ys_hbm.at[pl.ds(0, n)], buf, sem).wait()
    w = w_ref[...]
    y = w[:, 0:1] * buf[0:tm, :]
    for k in range(1, TOP_K):
        y = y + w[:, k:k + 1] * buf[k * tm:(k + 1) * tm, :]
    o_ref[...] = x_ref[...] + g_ref[...] * y


def moe_combine(x, ys, dest, top_w, mod):
    tm = COMBINE_TM
    nt = T_ALL // tm
    idx = jnp.transpose(dest.reshape(nt, tm, TOP_K), (0, 2, 1)).reshape(nt, 1, TOP_K * tm)
    return pl.pallas_call(
        _combine_kernel,
        out_shape=jax.ShapeDtypeStruct((T_ALL, D_MODEL), f32),
        grid=(nt,),
        in_specs=[pl.BlockSpec((None, 1, TOP_K * tm), lambda i: (i, 0, 0), memory_space=pltpu.SMEM),
                  pl.BlockSpec((tm, D_MODEL), lambda i: (i, 0)),
                  pl.BlockSpec((tm, LANE), lambda i: (i, 0)),
                  pl.BlockSpec((None, 1, D_MODEL), lambda i: (_cond_of_rows(i * tm) * 6 + 5, 0, 0)),
                  pl.BlockSpec(memory_space=pl.ANY)],
        out_specs=pl.BlockSpec((tm, D_MODEL), lambda i: (i, 0)),
        scratch_shapes=[pltpu.VMEM((TOP_K * tm, D_MODEL), f32), pltpu.SemaphoreType.DMA(())],
        compiler_params=_params(1), name="moe_combine",
    )(idx, x, top_w, mod, ys)


def kernel(x_prompt, x_sample, cache_attn_k, cache_attn_v, state_mlstm_C, state_mlstm_n, state_mlstm_m, c, c_ctx, ada_w, ada_b, norm1_g, norm2_g, w_in, ml_b_i, ml_b_f, ml_norm_g, da_lam, da_subln_g, hy_conv_w, hy_conv_b, hy_f_w1, hy_f_b1, hy_f_freq, hy_f_w2, hy_f_b2, hy_f_w3, hy_d, w_br_ml, w_br_da, w_br_hy, b_merge, w_out, router_w, router_b, moe_w_gate, moe_b_gate, moe_w_up, moe_b_up, moe_w_down, moe_b_down, final_g):
    x = jnp.concatenate([x_prompt.reshape(T_CTX, D_MODEL), x_sample.reshape(T_LAT, D_MODEL)], axis=0)
    cond = jnp.concatenate([c_ctx[None], c, jnp.zeros((COND_PAD - N_COND, D_MODEL), f32)], axis=0)
    mod_all = ada_modulation(cond, ada_w, ada_b).reshape(DEPTH, COND_PAD * 6, 1, D_MODEL)

    cos_t, sin_t = rope_tables(DEC_SEQ)
    dft = {L: dft_matrices(L) for L in (SEQ, DEC_SEQ)}
    cache_k = cache_attn_k.reshape(DEC_BATCH, DEPTH, PAST_LEN, DA_HEADS * 2 * DA_DH)
    cache_v = cache_attn_v.reshape(DEC_BATCH, DEPTH, PAST_LEN, DA_HEADS * DA_VD)
    lat_blk0 = T_CTX // DEC_SEQ

    new_k, new_v, new_C, new_n, new_m = [], [], [], [], []
    for l in range(DEPTH):
        lam_init = 0.8 - 0.6 * math.exp(-0.3 * l)
        mod = mod_all[l]
        h = norm_mod(x, norm1_g[l], mod, 1, 0)
        u_ml = in_projection(h, w_in, l, W_IN_ML, f32)
        u_gate = in_projection(h, w_in, l, W_IN_GATE, f32, tn=LANE)
        u_mix = in_projection(h, w_in, l, W_IN_MIX, f32)
        u_gl = in_projection(h, w_in, l, W_IN_GL, bf16)
        new_k.append(u_mix[:T_CTX, OFF_DK:OFF_DK + 1024].reshape(BATCH, SEQ, DA_HEADS, 2, DA_DH))
        new_v.append(u_mix[:T_CTX, OFF_DV:OFF_DV + 1024].reshape(BATCH, SEQ, DA_HEADS, DA_VD))

        g = u_gate[:, :4 * ML_HEADS].reshape(T_ALL, 4, ML_HEADS)
        gcol = jnp.transpose(g, (2, 0, 1))
        grow = jnp.transpose(g, (2, 1, 0))
        gb = jnp.concatenate([ml_b_i[l], ml_b_f[l]], axis=0)
        bcol = gb.T.reshape(ML_HEADS, 1, 4)
        brow = gb.T.reshape(ML_HEADS, 4, 1)
        ng = ml_norm_g[l].reshape(ML_HEADS, 1, ML_DV)
        hml_c, C_c, n_c, m_c = mlstm(u_ml, gcol, grow, bcol, brow, ng, nseq=BATCH, L=SEQ, row_blk0=0)
        init = (state_mlstm_C[:, l], state_mlstm_n[:, l].reshape(DEC_BATCH, 2, ML_HEADS, 1, ML_DK),
                jnp.broadcast_to(state_mlstm_m[:, l][..., None, None], (DEC_BATCH, 2, ML_HEADS, 1, LANE)))
        hml_l, _, _, _ = mlstm(u_ml, gcol, grow, bcol, brow, ng, nseq=DEC_BATCH, L=DEC_SEQ, row_blk0=lat_blk0,
                               init=init)
        h_ml = jnp.concatenate([hml_c, hml_l], axis=0)
        new_C.append(jnp.transpose(C_c, (0, 2, 1, 3, 4)))
        new_n.append(jnp.transpose(n_c, (0, 2, 1, 3)))
        new_m.append(jnp.transpose(m_c[..., 0], (0, 2, 1)))

        hda_c = diff_attention(u_mix, da_lam, da_subln_g, lam_init, layer=l, nseq=BATCH, L=SEQ, row_blk0=0, tq=SEQ)
        hda_l = diff_attention(u_mix, da_lam, da_subln_g, lam_init, layer=l, nseq=DEC_BATCH, L=DEC_SEQ,
                               row_blk0=lat_blk0, tq=256, cache=(cache_k, cache_v), rope=(cos_t, sin_t))
        h_da = jnp.concatenate([hda_c, hda_l], axis=0)

        pad_f = lambda a: jnp.pad(a, ((0, LANE - a.shape[0]), (0, LANE - a.shape[1])))
        w1 = pad_f(hy_f_w1[l])
        w2 = pad_f(hy_f_w2[l])
        w3 = jnp.pad(hy_f_w3[l], ((0, LANE - HY_FH), (0, 0)))
        b1 = jnp.pad(hy_f_b1[l], (0, LANE - HY_FH))[None]
        b2 = jnp.pad(hy_f_b2[l], (0, LANE - HY_FH))[None]
        fq = jnp.pad(hy_f_freq[l], ((0, 0), (0, LANE - HY_FH)))
        conv_w = jnp.transpose(hy_conv_w[l].reshape(3, 3, HY_WIDTH), (1, 0, 2))
        conv_b = hy_conv_b[l].reshape(3, 1, HY_WIDTH)
        h_hy_parts = []
        for (nseq, L, blk0) in ((BATCH, SEQ, 0), (DEC_BATCH, DEC_SEQ, lat_blk0)):
            fwd, inv = dft[L]
            hf = hyena_filter_spectrum(L, fwd, w1, b1, fq, w2, b2, w3)
            hf = jnp.transpose(hf.reshape(fwd.shape[0], 2, HY_WIDTH), (1, 0, 2))
            h_hy_parts.append(hyena(u_mix, conv_w, conv_b, hf, hy_d[l], fwd, inv, nseq=nseq, L=L, row_blk0=blk0))
        h_hy = jnp.concatenate(h_hy_parts, axis=0)

        merged = merge_branches(h_ml, h_da, h_hy, w_br_ml, w_br_da, w_br_hy, u_gl, b_merge, l)
        x = out_projection(merged, w_out, x, mod, l)

        h2, top_i, top_w = norm_router(x, norm2_g[l], mod, router_w[l], router_b[l])
        dest, tok_of_slot, tile_expert, n_used = route(top_i[:, :TOP_K])
        xs = gather_token_rows(h2, tok_of_slot, n_used)
        act = expert_up(xs, moe_w_gate, moe_b_gate, moe_w_up, moe_b_up, tile_expert, n_used, l)
        ys = expert_down(act, moe_w_down, moe_b_down, tile_expert, n_used, l)
        x = moe_combine(x, ys, dest, top_w, mod)

    y_prompt = final_norm(x, final_g, 0, T_CTX).reshape(BATCH, SEQ, D_MODEL)
    y_sample = final_norm(x, final_g, T_CTX, T_LAT).reshape(DEC_BATCH, DEC_SEQ, D_MODEL)
    return (y_prompt, y_sample, jnp.stack(new_k, axis=1), jnp.stack(new_v, axis=1),
            jnp.stack(new_C, axis=1), jnp.stack(new_n, axis=1), jnp.stack(new_m, axis=1))
```

```python
import functools
import math

import jax
import jax.numpy as jnp
from jax import lax
from jax.experimental import pallas as pl
from jax.experimental.pallas import tpu as pltpu

f32 = jnp.float32
bf16 = jnp.bfloat16

D_MODEL = 4096
BATCH = 32
SEQ = 256
DEPTH = 2
DEC_BATCH = 2
DEC_SEQ = 1024
PAST_LEN = 256
GRID_W = 64
EPS = 1e-6
ML_HEADS = 8
ML_DK = 128
ML_DV = 128
DA_HEADS = 8
DA_DH = 64
DA_VD = 128
ROPE_BASE = 10000.0
HY_WIDTH = 1024
HY_BANDS = 8
HY_EMB = 1 + 2 * HY_BANDS
HY_FH = 64
HY_TARGET = 1e-2
HY_MIN_DECAY = math.log(HY_TARGET) / 1.5
HY_MAX_DECAY = math.log(HY_TARGET) / 0.3
N_BRANCH = 3
N_EXPERTS = 32
TOP_K = 4
D_EXPERT = D_MODEL // 4
SWIGLU_ALPHA = 1.702
SWIGLU_LIMIT = 7.0

T_CTX = BATCH * SEQ
T_LAT = DEC_BATCH * DEC_SEQ
T_ALL = T_CTX + T_LAT
N_COND = 1 + DEC_BATCH
COND_PAD = 8

W_IN_ML = (0, 4096)
W_IN_GATE = (4096, 128)
W_IN_MIX = (4128, 6144)
W_IN_GL = (10272, 12288)
N_IN = 22560
OFF_MQ, OFF_MK, OFF_MV, OFF_MO = 0, 1024, 2048, 3072
OFF_DQ, OFF_DK, OFF_DV = 0, 1024, 2048
OFF_HU = 3072

LANE = 128
VMEM_LIMIT = 56 * 1024 * 1024

ML_CHUNK = 256
MOE_TILE = 512
MOE_NT = (T_ALL * TOP_K) // MOE_TILE + N_EXPERTS
MOE_ROWS = MOE_NT * MOE_TILE


def _params(n_axes):
    return pltpu.CompilerParams(dimension_semantics=("arbitrary",) * n_axes,
                                vmem_limit_bytes=VMEM_LIMIT)


def _cond_of_rows(row0):
    return jnp.where(row0 < T_CTX, 0, 1 + (row0 - T_CTX) // DEC_SEQ)


def _ada_kernel(c_ref, w_ref, b_ref, o_ref):
    c = c_ref[...]
    s = c * jax.nn.sigmoid(c)
    o_ref[...] = jnp.dot(s.astype(bf16), w_ref[...].astype(bf16),
                         preferred_element_type=f32) + b_ref[...]


def ada_modulation(cond, ada_w, ada_b):
    tn = 1024
    n = 6 * D_MODEL
    return pl.pallas_call(
        _ada_kernel,
        out_shape=jax.ShapeDtypeStruct((DEPTH, COND_PAD, n), f32),
        grid=(DEPTH, n // tn),
        in_specs=[pl.BlockSpec((COND_PAD, D_MODEL), lambda l, j: (0, 0)),
                  pl.BlockSpec((None, D_MODEL, tn), lambda l, j: (l, 0, j)),
                  pl.BlockSpec((None, 1, tn), lambda l, j: (l, 0, j))],
        out_specs=pl.BlockSpec((None, COND_PAD, tn), lambda l, j: (l, 0, j)),
        compiler_params=_params(2), name="ada_mod",
    )(cond, ada_w, ada_b.reshape(DEPTH, 1, n))


def _norm_mod_kernel(x_ref, g_ref, sc_ref, sh_ref, o_ref):
    x = x_ref[...]
    y = x * lax.rsqrt(jnp.mean(x * x, axis=-1, keepdims=True) + EPS) * g_ref[...]
    o_ref[...] = (y * (1.0 + sc_ref[...]) + sh_ref[...]).astype(o_ref.dtype)


def norm_mod(x, g, mod, j_scale, j_shift):
    tm = 512
    return pl.pallas_call(
        _norm_mod_kernel,
        out_shape=jax.ShapeDtypeStruct((T_ALL, D_MODEL), bf16),
        grid=(T_ALL // tm,),
        in_specs=[pl.BlockSpec((tm, D_MODEL), lambda i: (i, 0)),
                  pl.BlockSpec((1, D_MODEL), lambda i: (0, 0)),
                  pl.BlockSpec((None, 1, D_MODEL), lambda i: (_cond_of_rows(i * tm) * 6 + j_scale, 0, 0)),
                  pl.BlockSpec((None, 1, D_MODEL), lambda i: (_cond_of_rows(i * tm) * 6 + j_shift, 0, 0))],
        out_specs=pl.BlockSpec((tm, D_MODEL), lambda i: (i, 0)),
        compiler_params=_params(1), name="norm_mod",
    )(x, g.reshape(1, D_MODEL), mod, mod)


def _final_norm_kernel(x_ref, g_ref, o_ref):
    x = x_ref[...]
    o_ref[...] = x * lax.rsqrt(jnp.mean(x * x, axis=-1, keepdims=True) + EPS) * g_ref[...]


def final_norm(x, g, row0, nrows):
    tm = 512
    return pl.pallas_call(
        _final_norm_kernel,
        out_shape=jax.ShapeDtypeStruct((nrows, D_MODEL), f32),
        grid=(nrows // tm,),
        in_specs=[pl.BlockSpec((tm, D_MODEL), lambda i: (row0 // tm + i, 0)),
                  pl.BlockSpec((1, D_MODEL), lambda i: (0, 0))],
        out_specs=pl.BlockSpec((tm, D_MODEL), lambda i: (i, 0)),
        compiler_params=_params(1), name="final_norm",
    )(x, g.reshape(1, D_MODEL))


def _in_proj_kernel(a_ref, wt_ref, o_ref):
    o_ref[...] = lax.dot_general(a_ref[...], wt_ref[0].astype(bf16), (((1,), (1,)), ((), ())),
                                 preferred_element_type=f32).astype(o_ref.dtype)


def in_projection(h, w_in_t, layer, cols, out_dtype, tn=256):
    start, width = cols
    assert start % 8 == 0 and width % tn == 0
    tm = 2048
    return pl.pallas_call(
        _in_proj_kernel,
        out_shape=jax.ShapeDtypeStruct((T_ALL, width), out_dtype),
        grid=(T_ALL // tm, width // tn),
        in_specs=[pl.BlockSpec((tm, D_MODEL), lambda i, j: (i, 0), pipeline_mode=pl.Buffered(1)),
                  pl.BlockSpec((pl.Element(1), pl.Element(tn), pl.Element(D_MODEL)),
                               lambda i, j: (layer, pl.multiple_of(start + j * tn, 8), 0))],
        out_specs=pl.BlockSpec((tm, tn), lambda i, j: (i, j)),
        compiler_params=_params(2), name="in_proj",
    )(h, w_in_t)


def _log_sigmoid(x):
    return jnp.minimum(x, 0.0) - jnp.log(1.0 + jnp.exp(-jnp.abs(x)))


def _mlstm_chunk(q, k, v, i_col, f_col, i_row, f_row, C, n, m, mask_self, mask_other):
    b_col = jnp.sum(jnp.where(mask_self, f_row, 0.0), axis=1, keepdims=True)
    b_row = jnp.sum(jnp.where(mask_other, f_col, 0.0), axis=0, keepdims=True)
    total = jnp.sum(f_col, axis=0, keepdims=True)
    log_d = jnp.where(mask_self, b_col - b_row + i_row, -jnp.inf)
    m_inter = b_col + m
    m_s = jnp.maximum(m_inter, jnp.max(log_d, axis=1, keepdims=True))
    qb = q.astype(bf16)
    kb = k.astype(bf16)
    vb = v.astype(bf16)
    s = lax.dot_general(qb, kb, (((1,), (1,)), ((), ())), preferred_element_type=f32)
    w_intra = s * jnp.exp(log_d - m_s)
    w_inter = jnp.exp(m_inter - m_s)
    num = (jnp.dot(w_intra.astype(bf16), vb, preferred_element_type=f32)
           + w_inter * jnp.dot(qb, C.astype(bf16), preferred_element_type=f32))
    den = jnp.sum(w_intra, axis=1, keepdims=True) + w_inter * jnp.sum(q * n, axis=1, keepdims=True)
    h = num / jnp.maximum(jnp.abs(den), jnp.exp(-m_s))
    log_w = total - b_col + i_col
    m_new = jnp.maximum(total + m, jnp.max(log_w, axis=0, keepdims=True))
    w = jnp.exp(log_w - m_new)
    carry = jnp.exp(total + m - m_new)
    kw = k * w
    C_new = carry * C + lax.dot_general(kw.astype(bf16), vb, (((0,), (0,)), ((), ())),
                                        preferred_element_type=f32)
    n_new = carry * n + jnp.sum(kw, axis=0, keepdims=True)
    return h, C_new, n_new, m_new


def _mlstm_kernel(*refs, nchunks, zero_init):
    if zero_init:
        (q_ref, k_ref, v_ref, o_ref, gc_ref, gr_ref, bc_ref, br_ref, ng_ref,
         h_out, C_out, n_out, m_out, hacc) = refs
    else:
        (q_ref, k_ref, v_ref, o_ref, gc_ref, gr_ref, bc_ref, br_ref, ng_ref, C0_ref, n0_ref, m0_ref,
         h_out, C_out, n_out, m_out, hacc) = refs
    CH = ML_CHUNK
    gc = gc_ref[...] + bc_ref[...]
    gr = gr_ref[...] + br_ref[...]
    row = lax.broadcasted_iota(jnp.int32, (CH, CH), 0)
    col = lax.broadcasted_iota(jnp.int32, (CH, CH), 1)
    lower = col <= row
    upper = col >= row
    scale = ML_DK ** -0.5
    for d in range(2):
        if zero_init:
            C = jnp.zeros((ML_DK, ML_DV), f32)
            n = jnp.zeros((1, ML_DK), f32)
            m = jnp.zeros((1, 1), f32)
        else:
            C = C0_ref[d]
            n = n0_ref[d]
            m = m0_ref[d][:, 0:1]
        mask_self, mask_other = (lower, upper) if d == 0 else (upper, lower)
        order = range(nchunks) if d == 0 else range(nchunks - 1, -1, -1)
        for c in order:
            sl = slice(c * CH, (c + 1) * CH)
            q = q_ref[sl, :]
            k = k_ref[sl, :] * scale
            v = v_ref[sl, :]
            i_col = gc[sl, d:d + 1]
            f_col = _log_sigmoid(gc[sl, 2 + d:3 + d])
            i_row = gr[d:d + 1, sl]
            f_row = _log_sigmoid(gr[2 + d:3 + d, sl])
            h, C, n, m = _mlstm_chunk(q, k, v, i_col, f_col, i_row, f_row, C, n, m, mask_self, mask_other)
            if d == 0:
                hacc[sl, :] = h
            else:
                hacc[sl, :] += h
        C_out[d] = C
        n_out[d:d + 1, :] = n
        m_out[d:d + 1, :] = jnp.broadcast_to(m, (1, LANE))
    hg = jax.nn.sigmoid(o_ref[...]) * hacc[...]
    y = hg * lax.rsqrt(jnp.mean(hg * hg, axis=-1, keepdims=True) + EPS) * ng_ref[...]
    h_out[...] = y.astype(h_out.dtype)


def mlstm(u, gcol, grow, bcol, brow, norm_g, *, nseq, L, row_blk0, init=None):
    H = ML_HEADS
    nchunks = L // ML_CHUNK
    cb = lambda off: off // LANE
    in_specs = [
        pl.BlockSpec((L, LANE), lambda b, h: (row_blk0 + b, cb(OFF_MQ) + h)),
        pl.BlockSpec((L, LANE), lambda b, h: (row_blk0 + b, cb(OFF_MK) + h)),
        pl.BlockSpec((L, LANE), lambda b, h: (row_blk0 + b, cb(OFF_MV) + h)),
        pl.BlockSpec((L, LANE), lambda b, h: (row_blk0 + b, cb(OFF_MO) + h)),
        pl.BlockSpec((None, L, 4), lambda b, h: (h, row_blk0 + b, 0)),
        pl.BlockSpec((None, 4, L), lambda b, h: (h, 0, row_blk0 + b)),
        pl.BlockSpec((None, 1, 4), lambda b, h: (h, 0, 0)),
        pl.BlockSpec((None, 4, 1), lambda b, h: (h, 0, 0)),
        pl.BlockSpec((None, 1, LANE), lambda b, h: (h, 0, 0)),
    ]
    args = [u, u, u, u, gcol, grow, bcol, brow, norm_g]
    if init is not None:
        C0, n0, m0 = init
        in_specs += [
            pl.BlockSpec((None, 2, None, ML_DK, ML_DV), lambda b, h: (b, 0, h, 0, 0)),
            pl.BlockSpec((None, 2, None, 1, ML_DK), lambda b, h: (b, 0, h, 0, 0)),
            pl.BlockSpec((None, 2, None, 1, LANE), lambda b, h: (b, 0, h, 0, 0)),
        ]
        args += [C0, n0, m0]
    out_shape = (jax.ShapeDtypeStruct((nseq * L, ML_HEADS * ML_DV), bf16),
                 jax.ShapeDtypeStruct((nseq, H, 2, ML_DK, ML_DV), f32),
                 jax.ShapeDtypeStruct((nseq, H, 2, ML_DK), f32),
                 jax.ShapeDtypeStruct((nseq, H, 2, LANE), f32))
    out_specs = (pl.BlockSpec((L, LANE), lambda b, h: (b, h)),
                 pl.BlockSpec((None, None, 2, ML_DK, ML_DV), lambda b, h: (b, h, 0, 0, 0)),
                 pl.BlockSpec((None, None, 2, ML_DK), lambda b, h: (b, h, 0, 0)),
                 pl.BlockSpec((None, None, 2, LANE), lambda b, h: (b, h, 0, 0)))
    return pl.pallas_call(
        functools.partial(_mlstm_kernel, nchunks=nchunks, zero_init=init is None),
        out_shape=out_shape, grid=(nseq, H), in_specs=in_specs, out_specs=out_specs,
        scratch_shapes=[pltpu.VMEM((L, ML_DV), f32)],
        compiler_params=_params(2), name="mlstm_ctx" if init is None else "mlstm_lat",
    )(*args)


def _rope(x, cos, sin_signed):
    lane = lax.broadcasted_iota(jnp.int32, x.shape, 1)
    partner = jnp.where((lane & 16) == 0, pltpu.roll(x, LANE - 16, axis=1), pltpu.roll(x, 16, axis=1))
    return x * cos + partner * sin_signed


def _attn_kernel(*refs, n_cache, lam_init):
    if n_cache:
        (q_ref, k_ref, v_ref, lam_ref, g_ref, ck_ref, cv_ref, cq_ref, sq_ref, ckk_ref, skk_ref,
         o_ref, k_all, v_all) = refs
    else:
        q_ref, k_ref, v_ref, lam_ref, g_ref, o_ref, k_all, v_all = refs

    @pl.when(pl.program_id(2) == 0)
    def _():
        if n_cache:
            k_all[0:n_cache, :] = ck_ref[...].astype(bf16)
            v_all[0:n_cache, :] = cv_ref[...].astype(bf16)
            k_all[n_cache:, :] = _rope(k_ref[...], ckk_ref[...], skk_ref[...]).astype(bf16)
        else:
            k_all[...] = k_ref[...].astype(bf16)
        v_all[n_cache:, :] = v_ref[...].astype(bf16)

    q = q_ref[...]
    if n_cache:
        q = _rope(q, cq_ref[...], sq_ref[...])
    q = q * (DA_DH ** -0.5)
    lane = lax.broadcasted_iota(jnp.int32, q.shape, 1)
    lp = lam_ref[...]
    lam = (jnp.exp(jnp.sum(lp[0:1] * lp[1:2], axis=1, keepdims=True))
           - jnp.exp(jnp.sum(lp[2:3] * lp[3:4], axis=1, keepdims=True)) + lam_init)
    kb = k_all[...]
    probs = []
    for c in range(2):
        qc = jnp.where((lane >= c * DA_DH) & (lane < (c + 1) * DA_DH), q, 0.0).astype(bf16)
        s = lax.dot_general(qc, kb, (((1,), (1,)), ((), ())), preferred_element_type=f32)
        p = jnp.exp(s - jnp.max(s, axis=1, keepdims=True))
        probs.append(p / jnp.sum(p, axis=1, keepdims=True))
    a = probs[0] - lam * probs[1]
    o = jnp.dot(a.astype(bf16), v_all[...], preferred_element_type=f32)
    y = o * lax.rsqrt(jnp.mean(o * o, axis=-1, keepdims=True) + EPS) * g_ref[...]
    o_ref[...] = (y * (1.0 - lam_init)).astype(o_ref.dtype)


def diff_attention(u, da_lam, subln_g, lam_init, *, layer, nseq, L, row_blk0, tq, cache=None, rope=None):
    nq = L // tq
    cb = lambda off: off // LANE
    n_cache = 0 if cache is None else PAST_LEN
    in_specs = [
        pl.BlockSpec((tq, LANE), lambda b, h, i: ((row_blk0 + b) * nq + i, cb(OFF_DQ) + h)),
        pl.BlockSpec((L, LANE), lambda b, h, i: (row_blk0 + b, cb(OFF_DK) + h)),
        pl.BlockSpec((L, LANE), lambda b, h, i: (row_blk0 + b, cb(OFF_DV) + h)),
        pl.BlockSpec((None, 4, DA_DH), lambda b, h, i: (layer, 0, 0)),
        pl.BlockSpec((None, 1, DA_VD), lambda b, h, i: (layer, 0, 0)),
    ]
    args = [u, u, u, da_lam, subln_g.reshape(DEPTH, 1, DA_VD)]
    if cache is not None:
        ck, cv = cache
        cos, sin = rope
        in_specs += [
            pl.BlockSpec((None, None, PAST_LEN, LANE), lambda b, h, i: (b, layer, 0, h)),
            pl.BlockSpec((None, None, PAST_LEN, LANE), lambda b, h, i: (b, layer, 0, h)),
            pl.BlockSpec((tq, LANE), lambda b, h, i: (i, 0)),
            pl.BlockSpec((tq, LANE), lambda b, h, i: (i, 0)),
            pl.BlockSpec((L, LANE), lambda b, h, i: (0, 0)),
            pl.BlockSpec((L, LANE), lambda b, h, i: (0, 0)),
        ]
        args += [ck, cv, cos, sin, cos, sin]
    return pl.pallas_call(
        functools.partial(_attn_kernel, n_cache=n_cache, lam_init=lam_init),
        out_shape=jax.ShapeDtypeStruct((nseq * L, DA_HEADS * DA_VD), bf16),
        grid=(nseq, DA_HEADS, nq), in_specs=in_specs,
        out_specs=pl.BlockSpec((tq, LANE), lambda b, h, i: (b * nq + i, h)),
        scratch_shapes=[pltpu.VMEM((n_cache + L, LANE), bf16), pltpu.VMEM((n_cache + L, LANE), bf16)],
        compiler_params=_params(3), name="attn_ctx" if cache is None else "attn_lat",
    )(*args)


def rope_tables(L):
    pos = jnp.arange(L)
    row = (pos // GRID_W).astype(f32)
    colp = (pos % GRID_W).astype(f32)
    lane = jnp.arange(LANE)
    d = lane % DA_DH
    half = DA_DH // 4
    inv = ROPE_BASE ** (-(d % half).astype(f32) / half)
    p = jnp.where((d < DA_DH // 2)[None, :], row[:, None], colp[:, None])
    ang = p * inv[None, :]
    sign = jnp.where((d % (2 * half)) < half, -1.0, 1.0)
    return jnp.cos(ang), jnp.sin(ang) * sign[None, :]


def dft_matrices(L):
    n = 2 * L
    nf = L + 1
    nfp = L + 16
    k = jnp.arange(nfp)
    valid = (k < nf)
    s = jnp.arange(L)
    ang_f = (2.0 * math.pi / n) * ((k[:, None] * s[None, :]) % n).astype(f32)
    fr = jnp.where(valid[:, None], jnp.cos(ang_f), 0.0)
    fi = jnp.where(valid[:, None], -jnp.sin(ang_f), 0.0)
    fwd = jnp.concatenate([fr, fi], axis=0)
    t = jnp.arange(L) + L // 2
    ang_i = (2.0 * math.pi / n) * ((t[:, None] * k[None, :]) % n).astype(f32)
    wk = jnp.where((k == 0) | (k == L), 1.0, 2.0) / n
    wk = jnp.where(valid, wk, 0.0)
    inv = jnp.concatenate([jnp.cos(ang_i) * wk[None, :], -jnp.sin(ang_i) * wk[None, :]], axis=1)
    return fwd, inv


def _hy_filter_kernel(feat_ref, w1_ref, b1_ref, fq_ref, w2_ref, b2_ref, w3_ref, off_ref, dl_ref, fwd_ref, o_ref):
    hp = lax.Precision.HIGHEST
    fq = fq_ref[...]
    hdn = jnp.sin(fq[0:1] * (jnp.dot(feat_ref[...], w1_ref[...], precision=hp, preferred_element_type=f32)
                             + b1_ref[...]))
    hdn = jnp.sin(fq[1:2] * (jnp.dot(hdn, w2_ref[...], precision=hp, preferred_element_type=f32) + b2_ref[...]))
    filt = jnp.dot(hdn, w3_ref[...], precision=hp, preferred_element_type=f32)
    filt = filt * jnp.exp(-off_ref[...] * dl_ref[...])
    o_ref[...] = jnp.dot(fwd_ref[...], filt, precision=hp, preferred_element_type=f32)


def hyena_filter_spectrum(L, fwd, w1, b1, fq, w2, b2, w3):
    nfp2 = fwd.shape[0]
    pos = jnp.arange(L, dtype=f32)
    t = pos / (L - 1)
    ang = (2.0 * math.pi * pos / L)[:, None] * jnp.linspace(1e-4, HY_BANDS - 1, HY_BANDS, dtype=f32)
    feats = jnp.concatenate([t[:, None], jnp.cos(ang), -jnp.sin(ang)], axis=-1)
    feats = jnp.pad(feats, ((0, 0), (0, LANE - HY_EMB)))
    offset = (jnp.abs(pos - L // 2) / (L // 2))[:, None]
    deltas = jnp.abs(jnp.linspace(HY_MIN_DECAY, HY_MAX_DECAY, HY_WIDTH, dtype=f32))
    deltas = jnp.concatenate([deltas, deltas])[None, :]
    cbw = 256
    nw = 2 * HY_WIDTH
    full = lambda shape: pl.BlockSpec(shape, lambda j: (0,) * len(shape))
    return pl.pallas_call(
        _hy_filter_kernel,
        out_shape=jax.ShapeDtypeStruct((nfp2, nw), f32),
        grid=(nw // cbw,),
        in_specs=[full((L, LANE)), full((LANE, LANE)), full((1, LANE)), full((2, LANE)), full((LANE, LANE)),
                  full((1, LANE)), pl.BlockSpec((LANE, cbw), lambda j: (0, j)), full((L, 1)),
                  pl.BlockSpec((1, cbw), lambda j: (0, j)), full((nfp2, L))],
        out_specs=pl.BlockSpec((nfp2, cbw), lambda j: (0, j)),
        compiler_params=_params(1), name="hyena_filter",
    )(feats, w1, b1, fq, w2, b2, w3, offset, deltas, fwd)


def _hyena_kernel(v_ref, x1_ref, x2_ref, cw_ref, cbias_ref, hf_ref, d_ref, fwd_ref, inv_ref, o_ref, *, L):
    nfp = fwd_ref.shape[0] // 2
    cb = v_ref.shape[1]
    rows = lax.broadcasted_iota(jnp.int32, (L, cb), 0)

    def conv3(x_ref, part):
        x = x_ref[...]
        w = cw_ref[part]
        prev = jnp.where(rows == 0, 0.0, pltpu.roll(x, 1, axis=0))
        nxt = jnp.where(rows == L - 1, 0.0, pltpu.roll(x, L - 1, axis=0))
        return w[0:1] * prev + w[1:2] * x + w[2:3] * nxt + cbias_ref[part]

    z = conv3(v_ref, 0)
    gates = (conv3(x1_ref, 1), conv3(x2_ref, 2))
    fwd = fwd_ref[...]
    inv = inv_ref[...]
    dsk = d_ref[...]
    for o in range(2):
        zf = jnp.dot(fwd, z.astype(bf16), preferred_element_type=f32)
        zr, zi = zf[:nfp], zf[nfp:]
        hr, hi = hf_ref[o, :nfp, :], hf_ref[o, nfp:, :]
        yf = jnp.concatenate([zr * hr - zi * hi, zr * hi + zi * hr], axis=0).astype(bf16)
        y = jnp.dot(inv, yf, preferred_element_type=f32)
        z = gates[o] * (y + z * dsk[o:o + 1])
    o_ref[...] = z.astype(o_ref.dtype)


def hyena(u, conv_w, conv_b, hf, d_skip, fwd, inv, *, nseq, L, row_blk0):
    cbw = 256
    ncb = HY_WIDTH // cbw
    nfp2 = fwd.shape[0]
    ub = lambda part: (lambda b, j: (row_blk0 + b, (OFF_HU + part * HY_WIDTH) // cbw + j))
    return pl.pallas_call(
        functools.partial(_hyena_kernel, L=L),
        out_shape=jax.ShapeDtypeStruct((nseq * L, HY_WIDTH), bf16),
        grid=(nseq, ncb),
        in_specs=[pl.BlockSpec((L, cbw), ub(0)), pl.BlockSpec((L, cbw), ub(1)), pl.BlockSpec((L, cbw), ub(2)),
                  pl.BlockSpec((3, 3, cbw), lambda b, j: (0, 0, j)),
                  pl.BlockSpec((3, 1, cbw), lambda b, j: (0, 0, j)),
                  pl.BlockSpec((2, nfp2, cbw), lambda b, j: (0, 0, j)),
                  pl.BlockSpec((2, cbw), lambda b, j: (0, j)),
                  pl.BlockSpec((nfp2, L), lambda b, j: (0, 0)),
                  pl.BlockSpec((L, nfp2), lambda b, j: (0, 0))],
        out_specs=pl.BlockSpec((L, cbw), lambda b, j: (b, j)),
        compiler_params=_params(2), name="hyena_ctx" if L == SEQ else "hyena_lat",
    )(u, u, u, conv_w, conv_b, hf, d_skip, fwd.astype(bf16), inv.astype(bf16))


def _merge_kernel(a0, a1, a2, w0, w1, w2, g0, g1, g2, bm_ref, o_ref):
    acc = None
    for j, (a, w, g) in enumerate(((a0, w0, g0), (a1, w1, g1), (a2, w2, g2))):
        gate = jax.nn.sigmoid(g[...].astype(f32) + bm_ref[j])
        term = gate * jnp.dot(a[...], w[...].astype(bf16), preferred_element_type=f32)
        acc = term if acc is None else acc + term
    o_ref[...] = acc.astype(o_ref.dtype)


def merge_branches(h_ml, h_da, h_hy, w_ml, w_da, w_hy, u, b_merge, layer):
    tm, tn = 1024, 512
    a_spec = pl.BlockSpec((tm, 1024), lambda i, j: (i, 0))
    w_spec = pl.BlockSpec((None, 1024, tn), lambda i, j: (layer, 0, j))
    g_spec = lambda br: pl.BlockSpec((tm, tn), lambda i, j: (i, (br * D_MODEL) // tn + j))
    return pl.pallas_call(
        _merge_kernel,
        out_shape=jax.ShapeDtypeStruct((T_ALL, D_MODEL), bf16),
        grid=(T_ALL // tm, D_MODEL // tn),
        in_specs=[a_spec, a_spec, a_spec, w_spec, w_spec, w_spec, g_spec(0), g_spec(1), g_spec(2),
                  pl.BlockSpec((None, N_BRANCH, 1, tn), lambda i, j: (layer, 0, 0, j))],
        out_specs=pl.BlockSpec((tm, tn), lambda i, j: (i, j)),
        compiler_params=_params(2), name="merge",
    )(h_ml, h_da, h_hy, w_ml, w_da, w_hy, u, u, u, b_merge.reshape(DEPTH, N_BRANCH, 1, D_MODEL))


def _out_proj_kernel(a_ref, w_ref, x_ref, g_ref, o_ref):
    o_ref[...] = x_ref[...] + g_ref[...] * jnp.dot(a_ref[...], w_ref[...].astype(bf16),
                                                   preferred_element_type=f32)


def out_projection(merged, w_out, x, mod, layer):
    tm, tn = 1024, 256
    return pl.pallas_call(
        _out_proj_kernel,
        out_shape=jax.ShapeDtypeStruct((T_ALL, D_MODEL), f32),
        grid=(T_ALL // tm, D_MODEL // tn),
        in_specs=[pl.BlockSpec((tm, D_MODEL), lambda i, j: (i, 0)),
                  pl.BlockSpec((None, D_MODEL, tn), lambda i, j: (layer, 0, j)),
                  pl.BlockSpec((tm, tn), lambda i, j: (i, j)),
                  pl.BlockSpec((None, 1, tn), lambda i, j: (_cond_of_rows(i * tm) * 6 + 2, 0, j))],
        out_specs=pl.BlockSpec((tm, tn), lambda i, j: (i, j)),
        compiler_params=_params(2), name="out_proj",
    )(merged, w_out, x, mod)


def _norm_router_kernel(x_ref, g_ref, sc_ref, sh_ref, rw_ref, rb_ref, h_ref, ti_ref, tw_ref):
    x = x_ref[...]
    y = x * lax.rsqrt(jnp.mean(x * x, axis=-1, keepdims=True) + EPS) * g_ref[...]
    h = y * (1.0 + sc_ref[...]) + sh_ref[...]
    h_ref[...] = h
    logits =jnp.dot(h, rw_ref[...], precision=lax.Precision.HIGHEST, preferred_element_type=f32) + rb_ref[...]
    lane = lax.broadcasted_iota(jnp.int32, logits.shape, 1)
    logits = jnp.where(lane < N_EXPERTS, logits, -jnp.inf)
    idx_out = jnp.zeros(logits.shape, jnp.int32)
    val_out = jnp.zeros(logits.shape, f32)
    top = None
    for k in range(TOP_K):
        mx = jnp.max(logits, axis=1, keepdims=True)
        idx = jnp.min(jnp.where(logits == mx, lane, LANE), axis=1, keepdims=True)
        if top is None:
            top = mx
        idx_out = jnp.where(lane == k, idx, idx_out)
        val_out = jnp.where(lane == k, jnp.exp(mx - top), val_out)
        logits = jnp.where(lane == idx, -jnp.inf, logits)
    ti_ref[...] = idx_out
    tw_ref[...] = val_out / jnp.sum(val_out, axis=1, keepdims=True)


def norm_router(x, g, mod, router_w, router_b):
    tm = 256
    rw = jnp.pad(router_w, ((0, 0), (0, LANE - N_EXPERTS)))
    rb = jnp.pad(router_b, (0, LANE - N_EXPERTS)).reshape(1, LANE)
    row = pl.BlockSpec((tm, LANE), lambda i: (i, 0))
    return pl.pallas_call(
        _norm_router_kernel,
        out_shape=(jax.ShapeDtypeStruct((T_ALL, D_MODEL), f32),
                   jax.ShapeDtypeStruct((T_ALL, LANE), jnp.int32),
                   jax.ShapeDtypeStruct((T_ALL, LANE), f32)),
        grid=(T_ALL // tm,),
        in_specs=[pl.BlockSpec((tm, D_MODEL), lambda i: (i, 0)),
                  pl.BlockSpec((1, D_MODEL), lambda i: (0, 0)),
                  pl.BlockSpec((None, 1, D_MODEL), lambda i: (_cond_of_rows(i * tm) * 6 + 4, 0, 0)),
                  pl.BlockSpec((None, 1, D_MODEL), lambda i: (_cond_of_rows(i * tm) * 6 + 3, 0, 0)),
                  pl.BlockSpec((D_MODEL, LANE), lambda i: (0, 0)),
                  pl.BlockSpec((1, LANE), lambda i: (0, 0))],
        out_specs=(pl.BlockSpec((tm, D_MODEL), lambda i: (i, 0)), row, row),
        compiler_params=_params(1), name="norm_router",
    )(x, g.reshape(1, D_MODEL), mod, mod, rw, rb)


def route(top_i):
    eid = top_i.reshape(-1)
    onehot = (eid[:, None] == jnp.arange(N_EXPERTS)[None, :]).astype(jnp.int32)
    rank = jnp.take_along_axis(jnp.cumsum(onehot, axis=0) - onehot, eid[:, None], axis=1)[:, 0]
    counts = jnp.sum(onehot, axis=0)
    padded = ((counts + MOE_TILE - 1) // MOE_TILE) * MOE_TILE
    ends = jnp.cumsum(padded)
    dest = (ends - padded)[eid] + rank
    tok_of_slot = jnp.zeros((MOE_ROWS,), jnp.int32).at[dest].set(
        jnp.arange(eid.shape[0], dtype=jnp.int32) // TOP_K, mode="promise_in_bounds", unique_indices=True)
    tile_start = jnp.arange(MOE_NT, dtype=jnp.int32) * MOE_TILE
    tile_expert = jnp.minimum(jnp.sum((ends[None, :] <= tile_start[:, None]).astype(jnp.int32), axis=1),
                              N_EXPERTS - 1)
    n_used = (ends[-1] // MOE_TILE).astype(jnp.int32).reshape(1)
    return dest.astype(jnp.int32), tok_of_slot, tile_expert, n_used


def _gather_rows_kernel(nu_ref, idx_ref, nxt_ref, src_hbm, o_ref, buf, sem):
    n = o_ref.shape[0]
    i = pl.program_id(0)
    n_used = nu_ref[0]

    def start_tile(ids_ref, slot):
        def issue(r, carry):
            pltpu.make_async_copy(src_hbm.at[pl.ds(ids_ref[0, r], 1)], buf.at[slot, pl.ds(r, 1)],
                                  sem.at[slot]).start()
            return carry

        lax.fori_loop(0, n, issue, 0, unroll=8)

    @pl.when(i == 0)
    def _():
        start_tile(idx_ref, 0)

    @pl.when(i + 1 < n_used)
    def _():
        start_tile(nxt_ref, (i + 1) % 2)

    @pl.when(i < n_used)
    def _():
        slot = i % 2
        pltpu.make_async_copy(src_hbm.at[pl.ds(0, n)], buf.at[slot], sem.at[slot]).wait()
        o_ref[...] = buf[slot].astype(o_ref.dtype)

    @pl.when(i >= n_used)
    def _():
        o_ref[...] = jnp.zeros(o_ref.shape, o_ref.dtype)


def gather_token_rows(h, tok_of_slot, n_used):
    ids = tok_of_slot.reshape(MOE_NT, 1, MOE_TILE)
    ids_spec = lambda step: pl.BlockSpec((None, 1, MOE_TILE),
                                         lambda i, nu: (jnp.minimum(i + step, nu[0] - 1), 0, 0),
                                         memory_space=pltpu.SMEM)
    return pl.pallas_call(
        _gather_rows_kernel,
        out_shape=jax.ShapeDtypeStruct((MOE_ROWS, D_MODEL), bf16),
        grid_spec=pltpu.PrefetchScalarGridSpec(
            num_scalar_prefetch=1, grid=(MOE_NT,),
            in_specs=[ids_spec(0), ids_spec(1), pl.BlockSpec(memory_space=pl.ANY)],
            out_specs=pl.BlockSpec((MOE_TILE, D_MODEL), lambda i, nu: (i, 0)),
            scratch_shapes=[pltpu.VMEM((2, MOE_TILE, D_MODEL), f32), pltpu.SemaphoreType.DMA((2,))]),
        compiler_params=_params(1), name="moe_gather",
    )(n_used, ids, ids, h)


def _expert_up_kernel(te_ref, nu_ref, x_ref, wg_ref, bg_ref, wu_ref, bu_ref, o_ref):
    @pl.when(pl.program_id(1) < nu_ref[0])
    def _():
        x = x_ref[...]
        a_glu = jnp.dot(x, wg_ref[...].astype(bf16), preferred_element_type=f32) + bg_ref[...]
        a_lin = jnp.dot(x, wu_ref[...].astype(bf16), preferred_element_type=f32) + bu_ref[...]
        a_glu = jnp.minimum(a_glu, SWIGLU_LIMIT)
        a_lin = jnp.clip(a_lin, -SWIGLU_LIMIT, SWIGLU_LIMIT)
        o_ref[...] = (a_glu * jax.nn.sigmoid(SWIGLU_ALPHA * a_glu) * (a_lin + 1.0)).astype(o_ref.dtype)

    @pl.when(pl.program_id(1) >= nu_ref[0])
    def _():
        o_ref[...] = jnp.zeros(o_ref.shape, o_ref.dtype)


def expert_up(xs, w_gate, b_gate, w_up, b_up, tile_expert, n_used, layer):
    tn = 512
    tile = lambda j, r, te, nu: jnp.minimum(r, nu[0] - 1)
    w_spec = pl.BlockSpec((None, None, D_MODEL, tn), lambda j, r, te, nu: (layer, te[tile(j, r, te, nu)], 0, j))
    b_spec = pl.BlockSpec((None, None, 1, tn), lambda j, r, te, nu: (layer, te[tile(j, r, te, nu)], 0, j))
    return pl.pallas_call(
        _expert_up_kernel,
        out_shape=jax.ShapeDtypeStruct((MOE_ROWS, D_EXPERT), bf16),
        grid_spec=pltpu.PrefetchScalarGridSpec(
            num_scalar_prefetch=2, grid=(D_EXPERT // tn, MOE_NT),
            in_specs=[pl.BlockSpec((MOE_TILE, D_MODEL), lambda j, r, te, nu: (tile(j, r, te, nu), 0)),
                      w_spec, b_spec, w_spec, b_spec],
            out_specs=pl.BlockSpec((MOE_TILE, tn), lambda j, r, te, nu: (r, j))),
        compiler_params=_params(2), name="expert_up",
    )(tile_expert, n_used, xs, w_gate, b_gate.reshape(DEPTH, N_EXPERTS, 1, D_EXPERT),
      w_up, b_up.reshape(DEPTH, N_EXPERTS, 1, D_EXPERT))


def _expert_down_kernel(te_ref, nu_ref, a_ref, w_ref, b_ref, o_ref):
    @pl.when(pl.program_id(1) < nu_ref[0])
    def _():
        o_ref[...] = jnp.dot(a_ref[...], w_ref[...].astype(bf16), preferred_element_type=f32) + b_ref[...]

    @pl.when(pl.program_id(1) >= nu_ref[0])
    def _():
        o_ref[...] = jnp.zeros(o_ref.shape, o_ref.dtype)


def expert_down(act, w_down, b_down, tile_expert, n_used, layer):
    tn = 1024
    tile = lambda j, r, te, nu: jnp.minimum(r, nu[0] - 1)
    return pl.pallas_call(
        _expert_down_kernel,
        out_shape=jax.ShapeDtypeStruct((MOE_ROWS, D_MODEL), f32),
        grid_spec=pltpu.PrefetchScalarGridSpec(
            num_scalar_prefetch=2, grid=(D_MODEL // tn, MOE_NT),
            in_specs=[pl.BlockSpec((MOE_TILE, D_EXPERT), lambda j, r, te, nu: (tile(j, r, te, nu), 0)),
                      pl.BlockSpec((None, None, D_EXPERT, tn),
                                   lambda j, r, te, nu: (layer, te[tile(j, r, te, nu)], 0, j)),
                      pl.BlockSpec((None, None, 1, tn), lambda j, r, te, nu: (layer, te[tile(j, r, te, nu)], 0, j))],
            out_specs=pl.BlockSpec((MOE_TILE, tn), lambda j, r, te, nu: (r, j))),
        compiler_params=_params(2), name="expert_down",
    )(tile_expert, n_used, act, w_down, b_down.reshape(DEPTH, N_EXPERTS, 1, D_MODEL))


COMBINE_TM = 128


def _combine_kernel(idx_ref, x_ref, w_ref, g_ref, ys_hbm, o_ref, buf, sem):
    tm = COMBINE_TM
    n = TOP_K * tm

    def issue(i, carry):
        pltpu.make_async_copy(ys_hbm.at[pl.ds(idx_ref[0, i], 1)], buf.at[pl.ds(i, 1)], sem).start()
        return carry

    lax.fori_loop(0, n, issue, 0, unroll=8)
    pltpu.make_async_copy(ys_hbm.at[pl.ds(0, n)], buf, sem).wait()
    w = w_ref[...]
    y = w[:, 0:1] * buf[0:tm, :]
    for k in range(1, TOP_K):
        y = y + w[:, k:k + 1] * buf[k * tm:(k + 1) * tm, :]
    o_ref[...] = x_ref[...] + g_ref[...] * y


def moe_combine(x, ys, dest, top_w, mod):
    tm = COMBINE_TM
    nt = T_ALL // tm
    idx = jnp.transpose(dest.reshape(nt, tm, TOP_K), (0, 2, 1)).reshape(nt, 1, TOP_K * tm)
    return pl.pallas_call(
        _combine_kernel,
        out_shape=jax.ShapeDtypeStruct((T_ALL, D_MODEL), f32),
        grid=(nt,),
        in_specs=[pl.BlockSpec((None, 1, TOP_K * tm), lambda i: (i, 0, 0), memory_space=pltpu.SMEM),
                  pl.BlockSpec((tm, D_MODEL), lambda i: (i, 0)),
                  pl.BlockSpec((tm, LANE), lambda i: (i, 0)),
                  pl.BlockSpec((None, 1, D_MODEL), lambda i: (_cond_of_rows(i * tm) * 6 + 5, 0, 0)),
                  pl.BlockSpec(memory_space=pl.ANY)],
        out_specs=pl.BlockSpec((tm, D_MODEL), lambda i: (i, 0)),
        scratch_shapes=[pltpu.VMEM((TOP_K * tm, D_MODEL), f32), pltpu.SemaphoreType.DMA(())],
        compiler_params=_params(1), name="moe_combine",
    )(idx, x, top_w, mod, ys)


def kernel(x_prompt, x_sample, cache_attn_k, cache_attn_v, state_mlstm_C, state_mlstm_n, state_mlstm_m, c, c_ctx, ada_w, ada_b, norm1_g, norm2_g, w_in, ml_b_i, ml_b_f, ml_norm_g, da_lam, da_subln_g, hy_conv_w, hy_conv_b, hy_f_w1, hy_f_b1, hy_f_freq, hy_f_w2, hy_f_b2, hy_f_w3, hy_d, w_br_ml, w_br_da, w_br_hy, b_merge, w_out, router_w, router_b, moe_w_gate, moe_b_gate, moe_w_up, moe_b_up, moe_w_down, moe_b_down, final_g):
    x = jnp.concatenate([x_prompt.reshape(T_CTX, D_MODEL), x_sample.reshape(T_LAT, D_MODEL)], axis=0)
    cond = jnp.concatenate([c_ctx[None], c, jnp.zeros((COND_PAD - N_COND, D_MODEL), f32)], axis=0)
    mod_all = ada_modulation(cond, ada_w, ada_b).reshape(DEPTH, COND_PAD * 6, 1, D_MODEL)
    w_in_t = jnp.swapaxes(w_in, 1, 2)

    cos_t, sin_t = rope_tables(DEC_SEQ)
    dft = {L: dft_matrices(L) for L in (SEQ, DEC_SEQ)}
    cache_k = cache_attn_k.reshape(DEC_BATCH, DEPTH, PAST_LEN, DA_HEADS * 2 * DA_DH)
    cache_v = cache_attn_v.reshape(DEC_BATCH, DEPTH, PAST_LEN, DA_HEADS * DA_VD)
    lat_blk0 = T_CTX // DEC_SEQ

    new_k, new_v, new_C, new_n, new_m = [], [], [], [], []
    for l in range(DEPTH):
        lam_init = 0.8 - 0.6 * math.exp(-0.3 * l)
        mod = mod_all[l]
        h = norm_mod(x, norm1_g[l], mod, 1, 0)
        u_ml = in_projection(h, w_in_t, l, W_IN_ML, f32)
        u_gate = in_projection(h, w_in_t, l, W_IN_GATE, f32, tn=LANE)
        u_mix = in_projection(h, w_in_t, l, W_IN_MIX, f32)
        u_gl = in_projection(h, w_in_t, l, W_IN_GL, bf16)
        new_k.append(u_mix[:T_CTX, OFF_DK:OFF_DK + 1024].reshape(BATCH, SEQ, DA_HEADS, 2, DA_DH))
        new_v.append(u_mix[:T_CTX, OFF_DV:OFF_DV + 1024].reshape(BATCH, SEQ, DA_HEADS, DA_VD))

        g = u_gate[:, :4 * ML_HEADS].reshape(T_ALL, 4, ML_HEADS)
        gcol = jnp.transpose(g, (2, 0, 1))
        grow = jnp.transpose(g, (2, 1, 0))
        gb = jnp.concatenate([ml_b_i[l], ml_b_f[l]], axis=0)
        bcol = gb.T.reshape(ML_HEADS, 1, 4)
        brow = gb.T.reshape(ML_HEADS, 4, 1)
        ng = ml_norm_g[l].reshape(ML_HEADS, 1, ML_DV)
        hml_c, C_c, n_c, m_c = mlstm(u_ml, gcol, grow, bcol, brow, ng, nseq=BATCH, L=SEQ, row_blk0=0)
        init = (state_mlstm_C[:, l], state_mlstm_n[:, l].reshape(DEC_BATCH, 2, ML_HEADS, 1, ML_DK),
                jnp.broadcast_to(state_mlstm_m[:, l][..., None, None], (DEC_BATCH, 2, ML_HEADS, 1, LANE)))
        hml_l, _, _, _ = mlstm(u_ml, gcol, grow, bcol, brow, ng, nseq=DEC_BATCH, L=DEC_SEQ, row_blk0=lat_blk0,
                               init=init)
        h_ml = jnp.concatenate([hml_c, hml_l], axis=0)
        new_C.append(jnp.transpose(C_c, (0, 2, 1, 3, 4)))
        new_n.append(jnp.transpose(n_c, (0, 2, 1, 3)))
        new_m.append(jnp.transpose(m_c[..., 0], (0, 2, 1)))

        hda_c = diff_attention(u_mix, da_lam, da_subln_g, lam_init, layer=l, nseq=BATCH, L=SEQ, row_blk0=0, tq=SEQ)
        hda_l = diff_attention(u_mix, da_lam, da_subln_g, lam_init, layer=l, nseq=DEC_BATCH, L=DEC_SEQ,
                               row_blk0=lat_blk0, tq=256, cache=(cache_k, cache_v), rope=(cos_t, sin_t))
        h_da = jnp.concatenate([hda_c, hda_l], axis=0)

        pad_f = lambda a: jnp.pad(a, ((0, LANE - a.shape[0]), (0, LANE - a.shape[1])))
        w1 = pad_f(hy_f_w1[l])
        w2 = pad_f(hy_f_w2[l])
        w3 = jnp.pad(hy_f_w3[l], ((0, LANE - HY_FH), (0, 0)))
        b1 = jnp.pad(hy_f_b1[l], (0, LANE - HY_FH))[None]
        b2 = jnp.pad(hy_f_b2[l], (0, LANE - HY_FH))[None]
        fq = jnp.pad(hy_f_freq[l], ((0, 0), (0, LANE - HY_FH)))
        conv_w = jnp.transpose(hy_conv_w[l].reshape(3, 3, HY_WIDTH), (1, 0, 2))
        conv_b = hy_conv_b[l].reshape(3, 1, HY_WIDTH)
        h_hy_parts = []
        for (nseq, L, blk0) in ((BATCH, SEQ, 0), (DEC_BATCH, DEC_SEQ, lat_blk0)):
            fwd, inv = dft[L]
            hf = hyena_filter_spectrum(L, fwd, w1, b1, fq, w2, b2, w3)
            hf = jnp.transpose(hf.reshape(fwd.shape[0], 2, HY_WIDTH), (1, 0, 2))
            h_hy_parts.append(hyena(u_mix, conv_w, conv_b, hf, hy_d[l], fwd, inv, nseq=nseq, L=L, row_blk0=blk0))
        h_hy = jnp.concatenate(h_hy_parts, axis=0)

        merged = merge_branches(h_ml, h_da, h_hy, w_br_ml, w_br_da, w_br_hy, u_gl, b_merge, l)
        x = out_projection(merged, w_out, x, mod, l)

        h2, top_i, top_w = norm_router(x, norm2_g[l], mod, router_w[l], router_b[l])
        dest, tok_of_slot, tile_expert, n_used = route(top_i[:, :TOP_K])
        xs = gather_token_rows(h2, tok_of_slot, n_used)
        act = expert_up(xs, moe_w_gate, moe_b_gate, moe_w_up, moe_b_up, tile_expert, n_used, l)
        ys = expert_down(act, moe_w_down, moe_b_down, tile_expert, n_used, l)
        x = moe_combine(x, ys, dest, top_w, mod)

    y_prompt = final_norm(x, final_g, 0, T_CTX).reshape(BATCH, SEQ, D_MODEL)
    y_sample = final_norm(x, final_g, T_CTX, T_LAT).reshape(DEC_BATCH, DEC_SEQ, D_MODEL)
    return (y_prompt, y_sample, jnp.stack(new_k, axis=1), jnp.stack(new_v, axis=1),
            jnp.stack(new_C, axis=1), jnp.stack(new_n, axis=1), jnp.stack(new_m, axis=1))
```
